```python
import math
import jax, jax.numpy as jnp
from jax import lax
import numpy as np

D_MODEL = 1024
BATCH = 8
SEQ = 4096
DEPTH = 2
DEC_BATCH = 32
DEC_SEQ = 1
PAST_LEN = 16384
PAGE_SIZE = 128

N_A_LAYERS = (DEPTH + 1) // 2
N_B_LAYERS = DEPTH // 2
EPS = 1e-6

A_WIDTH = 2 * D_MODEL
A_CHUNK = 128
A_GROUPS = 16
A_GROUP_DIM = A_WIDTH // A_GROUPS

N_HEADS = 16
HEAD_DIM = D_MODEL // N_HEADS
N_KV = 4
Q_PER_KV = N_HEADS // N_KV
ROT_DIM = HEAD_DIM // 4
ROPE_THETA = 500000.0
CMP_BLOCK = 32
CMP_STRIDE = 16
CMP_HIDDEN = HEAD_DIM
SEL_BLOCK = 64
N_SELECT = 16
WINDOW = 512
NSA_Q_BLOCK = 32
N_BRANCH = 3
B_QW = N_HEADS * HEAD_DIM
B_KVW = 2 * N_KV * HEAD_DIM
B_IN = B_QW + N_BRANCH * B_KVW + N_BRANCH * B_QW + N_BRANCH * N_HEADS

kernel_name = 'hybrid_chunkmlp_nsa_decode_step'


def rmsnorm(x, g):
    xf = x.astype(jnp.float32)
    y = xf * lax.rsqrt(jnp.mean(xf * xf, axis=-1, keepdims=True) + EPS)
    return (y * g.astype(jnp.float32)).astype(x.dtype)


def layernorm(x, g, b):
    xf = x.astype(jnp.float32)
    xc = xf - jnp.mean(xf, axis=-1, keepdims=True)
    y = xc * lax.rsqrt(jnp.mean(xc * xc, axis=-1, keepdims=True) + EPS)
    return (y * g.astype(jnp.float32) + b.astype(jnp.float32)).astype(x.dtype)


def partial_rope(x, pos):
    half = ROT_DIM // 2
    freqs = jnp.exp(-math.log(ROPE_THETA) * jnp.arange(half, dtype=jnp.float32) * (2.0 / ROT_DIM))
    ang = pos.astype(jnp.float32)[:, None] * freqs[None, :]
    cos = jnp.cos(ang)[None, :, None, :]
    sin = jnp.sin(ang)[None, :, None, :]
    xr = x[..., :ROT_DIM].astype(jnp.float32)
    x1, x2 = xr[..., :half], xr[..., half:]
    rot = jnp.concatenate([x1 * cos - x2 * sin, x2 * cos + x1 * sin], axis=-1)
    return jnp.concatenate([rot.astype(x.dtype), x[..., ROT_DIM:]], axis=-1)


def masked_softmax(s, mask):
    s = jnp.where(mask, s, -jnp.inf)
    m = jnp.max(s, axis=-1, keepdims=True)
    m = jnp.where(jnp.isfinite(m), m, 0.0)
    e = jnp.where(mask, jnp.exp(s - m), 0.0)
    return e / jnp.maximum(jnp.sum(e, axis=-1, keepdims=True), jnp.finfo(jnp.float32).tiny)


def chunk_spatial_mix(v, w_s, b_s):
    B, L, _ = v.shape
    n_chunk = -(-L // A_CHUNK)
    pad = n_chunk * A_CHUNK - L
    vp = jnp.pad(v, ((0, 0), (0, pad), (0, 0))).reshape(B, n_chunk, A_CHUNK, A_GROUPS, A_GROUP_DIM)
    causal = jnp.tril(jnp.ones((A_CHUNK, A_CHUNK), dtype=bool))
    wm = jnp.where(causal[None], w_s, 0).astype(v.dtype)
    out = jnp.einsum('gts,bnsgc->bntgc', wm, vp) + b_s.T[None, None, :, :, None].astype(v.dtype)
    return out.reshape(B, n_chunk * A_CHUNK, A_WIDTH)[:, :L]


def chunk_mlp_mixer(xn, w_in, ln_g, ln_b, w_s, b_s, w_out):
    u, v, z = jnp.split(xn @ w_in, 3, axis=-1)
    u = jax.nn.gelu(u, approximate=False)
    v = layernorm(jax.nn.gelu(v, approximate=False), ln_g, ln_b)
    y = u * chunk_spatial_mix(v, w_s, b_s) * jax.nn.silu(z)
    return y @ w_out, v


def nsa_project(xn, w_in, pos):
    B, S, _ = xn.shape
    h = xn @ w_in
    o1 = B_QW
    o2 = o1 + N_BRANCH * B_KVW
    o3 = o2 + N_BRANCH * B_QW
    q = h[..., :o1].reshape(B, S, N_HEADS, HEAD_DIM)
    kv = h[..., o1:o2].reshape(B, S, N_BRANCH, 2, N_KV, HEAD_DIM)
    z = h[..., o2:o3].reshape(B, S, N_BRANCH, N_HEADS, HEAD_DIM)
    gl = h[..., o3:].reshape(B, S, N_BRANCH, N_HEADS)
    kv_c = kv[:, :, 0]
    kv_s = jnp.stack([partial_rope(kv[:, :, 1, 0], pos), kv[:, :, 1, 1]], axis=2)
    kv_w = jnp.stack([partial_rope(kv[:, :, 2, 0], pos), kv[:, :, 2, 1]], axis=2)
    return q, partial_rope(q, pos), kv_c, kv_s, kv_w, z, gl


def compress(k, pe, w1, b1, w2, b2):
    B, L = k.shape[:2]
    n_half = L // CMP_STRIDE
    kh = k[:, :n_half * CMP_STRIDE].reshape(B, n_half, CMP_STRIDE, N_KV, HEAD_DIM)
    lo = jnp.einsum('bmjgc,jch->bmgh', kh + pe[None, None, :CMP_STRIDE, None, :], w1[:CMP_STRIDE])
    hi = jnp.einsum('bmjgc,jch->bmgh', kh + pe[None, None, CMP_STRIDE:, None, :], w1[CMP_STRIDE:])
    hid = jax.nn.silu(lo[:, :-1] + hi[:, 1:] + b1)
    return hid @ w2 + b2


def to_sel_blocks(kv):
    B, L = kv.shape[:2]
    ns = -(-L // SEL_BLOCK)
    kvp = jnp.pad(kv, ((0, 0), (0, ns * SEL_BLOCK - L), (0, 0), (0, 0), (0, 0)))
    kvb = kvp.reshape(B, ns, SEL_BLOCK, 2, N_KV, HEAD_DIM)
    return kvb[:, :, :, 0].transpose(0, 3, 1, 2, 4), kvb[:, :, :, 1].transpose(0, 3, 1, 2, 4)


def gather_blocks(blk, idx):
    return jax.vmap(jax.vmap(lambda b_, i_: b_[i_]))(blk, idx)


def nsa_attend(q_c, q_r, pos_q, kc, vc, ks_blk, vs_blk, kw, vw, pos_kw):
    B, Q = q_c.shape[:2]
    scale = HEAD_DIM ** -0.5
    qc = q_c.reshape(B, Q, N_KV, Q_PER_KV, HEAD_DIM)
    qr = q_r.reshape(B, Q, N_KV, Q_PER_KV, HEAD_DIM)
    n_cmp = kc.shape[1]
    cmp_start = jnp.arange(n_cmp, dtype=jnp.int32) * CMP_STRIDE
    cmp_end = cmp_start + CMP_BLOCK - 1
    mask_c = cmp_end[None, :] <= pos_q[:, None]
    s_c = jnp.einsum('bqgrd,bngd->bgrqn', qc, kc).astype(jnp.float32) * scale
    p_c = masked_softmax(s_c, mask_c)
    o_c = jnp.einsum('bgrqn,bngd->bqgrd', p_c.astype(vc.dtype), vc)
    n_blk = ks_blk.shape[2]
    sel_start = jnp.arange(n_blk, dtype=jnp.int32) * SEL_BLOCK
    overlap = ((cmp_start[:, None] <= sel_start[None, :] + SEL_BLOCK - 1)
               & (cmp_end[:, None] >= sel_start[None, :])).astype(jnp.float32)
    imp = jnp.einsum('bgqn,nj->bgqj', jnp.sum(p_c, axis=2), overlap)
    blk = jnp.arange(n_blk, dtype=jnp.int32)[None, :]
    jq = (pos_q // SEL_BLOCK)[:, None]
    forced = (blk == 0) | (blk == jq) | (blk == jq - 1)
    score = jnp.where(blk > jq, -jnp.inf, jnp.where(forced, jnp.inf, imp))
    top_s, top_i = lax.top_k(score, min(N_SELECT, n_blk))
    n_s = top_i.shape[-1]
    k_g = gather_blocks(ks_blk, top_i).reshape(B, N_KV, Q, n_s * SEL_BLOCK, HEAD_DIM)
    v_g = gather_blocks(vs_blk, top_i).reshape(B, N_KV, Q, n_s * SEL_BLOCK, HEAD_DIM)
    pos_sel = top_i[..., None] * SEL_BLOCK + jnp.arange(SEL_BLOCK, dtype=jnp.int32)
    mask_s = (top_s > -jnp.inf)[..., None] & (pos_sel <= pos_q[None, None, :, None, None])
    mask_s = mask_s.reshape(B, N_KV, 1, Q, n_s * SEL_BLOCK)
    s_s = jnp.einsum('bqgrd,bgqkd->bgrqk', qr, k_g).astype(jnp.float32) * scale
    p_s = masked_softmax(s_s, mask_s)
    o_s = jnp.einsum('bgrqk,bgqkd->bqgrd', p_s.astype(v_g.dtype), v_g)
    dist = pos_q[:, None] - pos_kw[None, :]
    mask_w = (pos_kw[None, :] >= 0) & (dist >= 0) & (dist < WINDOW)
    s_w = jnp.einsum('bqgrd,bkgd->bgrqk', qr, kw).astype(jnp.float32) * scale
    p_w = masked_softmax(s_w, mask_w)
    o_w = jnp.einsum('bgrqk,bkgd->bqgrd', p_w.astype(vw.dtype), vw)
    shp = (B, Q, N_HEADS, HEAD_DIM)
    return jnp.stack([o_c.reshape(shp), o_s.reshape(shp), o_w.reshape(shp)], axis=2)


def nsa_merge(o, z, gl, gate_b, w_out):
    B, S = o.shape[:2]
    g = jax.nn.sigmoid(gl + gate_b)
    y = jnp.sum(o * jax.nn.silu(z) * g[..., None], axis=2).reshape(B, S, B_QW)
    return y @ w_out


def nsa_prompt(xn, w_in, pe, w1, b1, w2, b2, gate_b, w_out):
    B, S, _ = xn.shape
    pos = jnp.arange(S, dtype=jnp.int32)
    q, q_r, kv_c, kv_s, kv_w, z, gl = nsa_project(xn, w_in, pos)
    kc = compress(kv_c[:, :, 0], pe[0], w1[0], b1[0], w2[0], b2[0])
    vc = compress(kv_c[:, :, 1], pe[1], w1[1], b1[1], w2[1], b2[1])
    ks_blk, vs_blk = to_sel_blocks(kv_s)
    kw_pad = jnp.pad(kv_w, ((0, 0), (WINDOW, 0), (0, 0), (0, 0), (0, 0)))
    band = WINDOW + NSA_Q_BLOCK

    def query_block(s0):
        qb = lax.dynamic_slice_in_dim(q, s0, NSA_Q_BLOCK, axis=1)
        qrb = lax.dynamic_slice_in_dim(q_r, s0, NSA_Q_BLOCK, axis=1)
        wb = lax.dynamic_slice_in_dim(kw_pad, s0, band, axis=1)
        pos_q = s0 + jnp.arange(NSA_Q_BLOCK, dtype=jnp.int32)
        pos_kw = s0 - WINDOW + jnp.arange(band, dtype=jnp.int32)
        return nsa_attend(qb, qrb, pos_q, kc, vc, ks_blk, vs_blk, wb[:, :, 0], wb[:, :, 1], pos_kw)

    starts = jnp.arange(S // NSA_Q_BLOCK, dtype=jnp.int32) * NSA_Q_BLOCK
    o = lax.map(query_block, starts)
    o = jnp.moveaxis(o, 0, 1).reshape(B, S, N_BRANCH, N_HEADS, HEAD_DIM)
    y = nsa_merge(o, z, gl, gate_b, w_out)
    return y, kv_c, kv_s, kv_w[:, S - min(WINDOW, S):]


def nsa_sample(xn, pool_c, pool_s, win, page_table, w_in, pe, w1, b1, w2, b2, gate_b, w_out):
    B, T, _ = xn.shape
    past = page_table.shape[1] * PAGE_SIZE
    pos = past + jnp.arange(T, dtype=jnp.int32)
    q, q_r, kv_c, kv_s, kv_w, z, gl = nsa_project(xn, w_in, pos)
    old_c = pool_c[page_table].reshape(B, past, 2, N_KV, HEAD_DIM)
    old_s = pool_s[page_table].reshape(B, past, 2, N_KV, HEAD_DIM)
    full_c = jnp.concatenate([old_c, kv_c], axis=1)
    full_s = jnp.concatenate([old_s, kv_s], axis=1)
    kc = compress(full_c[:, :, 0], pe[0], w1[0], b1[0], w2[0], b2[0])
    vc = compress(full_c[:, :, 1], pe[1], w1[1], b1[1], w2[1], b2[1])
    ks_blk, vs_blk = to_sel_blocks(full_s)
    wlen = win.shape[1]
    kvw = jnp.concatenate([win, kv_w], axis=1)
    pos_kw = past - wlen + jnp.arange(wlen + T, dtype=jnp.int32)
    o = nsa_attend(q, q_r, pos, kc, vc, ks_blk, vs_blk, kvw[:, :, 0], kvw[:, :, 1], pos_kw)
    y = nsa_merge(o, z, gl, gate_b, w_out)
    return y, kv_c, kv_s, kvw[:, kvw.shape[1] - wlen:]


def setup_inputs(seed: int = 0) -> dict:
    key = jax.random.key(seed)
    ks = jax.random.split(key, 24)
    f32 = jnp.float32

    def nrm(k, shape, s=1.0):
        return s * jax.random.normal(k, shape, f32)

    n_pages = PAST_LEN // PAGE_SIZE
    n_used = DEC_BATCH * n_pages
    n_phys = n_used + max(1, n_used // 4)
    win_len = min(WINDOW, PAST_LEN)
    page_table = jax.random.permutation(ks[5], n_phys)[:n_used].reshape(DEC_BATCH, n_pages).astype(jnp.int32)
    return {
        'x_prompt': nrm(ks[0], (BATCH, SEQ, D_MODEL)),
        'x_sample': nrm(ks[1], (DEC_BATCH, DEC_SEQ, D_MODEL)),
        'cache_cmp_kv': nrm(ks[2], (N_B_LAYERS, n_phys, PAGE_SIZE, 2, N_KV, HEAD_DIM)),
        'cache_sel_kv': nrm(ks[3], (N_B_LAYERS, n_phys, PAGE_SIZE, 2, N_KV, HEAD_DIM)),
        'state_win_kv': nrm(ks[4], (N_B_LAYERS, DEC_BATCH, win_len, 2, N_KV, HEAD_DIM)),
        'page_table': page_table,
        'norm_g': 1.0 + nrm(ks[6], (DEPTH, D_MODEL), 0.1),
        'final_norm_g': 1.0 + nrm(ks[7], (D_MODEL,), 0.1),
        'a_w_in': nrm(ks[8], (N_A_LAYERS, D_MODEL, 3 * A_WIDTH), D_MODEL ** -0.5),
        'a_ln_g': 1.0 + nrm(ks[9], (N_A_LAYERS, A_WIDTH), 0.1),
        'a_ln_b': nrm(ks[10], (N_A_LAYERS, A_WIDTH), 0.1),
        'a_w_s': nrm(ks[11], (N_A_LAYERS, A_GROUPS, A_CHUNK, A_CHUNK), 0.5 * A_CHUNK ** -0.5),
        'a_b_s': 1.0 + nrm(ks[12], (N_A_LAYERS, A_GROUPS, A_CHUNK), 0.1),
        'a_w_out': nrm(ks[13], (N_A_LAYERS, A_WIDTH, D_MODEL), A_WIDTH ** -0.5),
        'b_w_in': nrm(ks[14], (N_B_LAYERS, D_MODEL, B_IN), D_MODEL ** -0.5),
        'b_cmp_pe': nrm(ks[15], (N_B_LAYERS, 2, CMP_BLOCK, HEAD_DIM), 0.5),
        'b_cmp_w1': nrm(ks[16], (N_B_LAYERS, 2, CMP_BLOCK, HEAD_DIM, CMP_HIDDEN), (CMP_BLOCK * HEAD_DIM) ** -0.5),
        'b_cmp_b1': nrm(ks[17], (N_B_LAYERS, 2, CMP_HIDDEN), 0.1),
        'b_cmp_w2': nrm(ks[18], (N_B_LAYERS, 2, CMP_HIDDEN, HEAD_DIM), CMP_HIDDEN ** -0.5),
        'b_cmp_b2': nrm(ks[19], (N_B_LAYERS, 2, HEAD_DIM), 0.1),
        'b_gate_b': nrm(ks[20], (N_B_LAYERS, N_BRANCH, N_HEADS), 0.1),
        'b_w_out': nrm(ks[21], (N_B_LAYERS, B_QW, D_MODEL), B_QW ** -0.5),
    }


def reference(x_prompt, x_sample, cache_cmp_kv, cache_sel_kv, state_win_kv, page_table,
              norm_g, final_norm_g, a_w_in, a_ln_g, a_ln_b, a_w_s, a_b_s, a_w_out,
              b_w_in, b_cmp_pe, b_cmp_w1, b_cmp_b1, b_cmp_w2, b_cmp_b2, b_gate_b, b_w_out):
    xp, xs = x_prompt, x_sample
    chunk_v_s, cmp_p, cmp_s, sel_p, sel_s, win_p, win_s = [], [], [], [], [], [], []
    for i in range(DEPTH):
        j = i // 2
        hp = rmsnorm(xp, norm_g[i])
        hs = rmsnorm(xs, norm_g[i])
        if i % 2 == 0:
            yp, _ = chunk_mlp_mixer(hp, a_w_in[j], a_ln_g[j], a_ln_b[j], a_w_s[j], a_b_s[j], a_w_out[j])
            ys, v_new = chunk_mlp_mixer(hs, a_w_in[j], a_ln_g[j], a_ln_b[j], a_w_s[j], a_b_s[j], a_w_out[j])
            chunk_v_s.append(v_new)
        else:
            yp, c_p, s_p, w_p = nsa_prompt(hp, b_w_in[j], b_cmp_pe[j], b_cmp_w1[j], b_cmp_b1[j],
                                           b_cmp_w2[j], b_cmp_b2[j], b_gate_b[j], b_w_out[j])
            ys, c_s, s_s, w_s = nsa_sample(hs, cache_cmp_kv[j], cache_sel_kv[j], state_win_kv[j], page_table,
                                           b_w_in[j], b_cmp_pe[j], b_cmp_w1[j], b_cmp_b1[j],
                                           b_cmp_w2[j], b_cmp_b2[j], b_gate_b[j], b_w_out[j])
            cmp_p.append(c_p)
            cmp_s.append(c_s)
            sel_p.append(s_p)
            sel_s.append(s_s)
            win_p.append(w_p)
            win_s.append(w_s)
        xp = xp + yp
        xs = xs + ys
    y_prompt = rmsnorm(xp, final_norm_g)
    y_sample = rmsnorm(xs, final_norm_g)
    return (y_prompt, y_sample, jnp.stack(cmp_p), jnp.stack(cmp_s), jnp.stack(sel_p), jnp.stack(sel_s),
            jnp.stack(win_p), jnp.stack(win_s), jnp.stack(chunk_v_s))
```

```python
import functools
import math

import numpy as np
import jax
import jax.numpy as jnp
from jax import lax
from jax.experimental import pallas as pl
from jax.experimental.pallas import tpu as pltpu

F32 = jnp.float32
BF16 = jnp.bfloat16
EPS = 1e-6

A_CHUNK = 128
A_GROUPS = 16
N_HEADS = 16
HEAD_DIM = 64
N_KV = 4
Q_PER_KV = N_HEADS // N_KV
ROT_DIM = HEAD_DIM // 4
ROPE_THETA = 500000.0
CMP_BLOCK = 32
CMP_STRIDE = 16
SEL_BLOCK = 64
N_SELECT = 16
WINDOW = 512
N_BRANCH = 3
PAGE_SIZE = 128

LANE = 128
SLOT = LANE
KVW = N_KV * HEAD_DIM
KVROW = 2 * KVW
TM_ROWS = 256
TQ = 256
PAGES_PER_STEP = 32
VMEM_LIMIT = 56 * 1024 * 1024
MASK_BIG = float(2.0 ** 127)
SQRT_HALF = float(np.sqrt(0.5))
NEG_INF = float("-inf")


def _rms(x, g):
    return x * lax.rsqrt(jnp.mean(x * x, axis=-1, keepdims=True) + EPS) * g


def _gelu(x):
    return 0.5 * x * (1.0 + lax.erf(x * SQRT_HALF))


def _silu(x):
    return x * jax.nn.sigmoid(x)


def _dot(a, b):
    return jnp.dot(a, b, preferred_element_type=F32)


def _dot_nt(a, b):
    return lax.dot_general(a, b, (((1,), (1,)), ((), ())), preferred_element_type=F32)


def _split3(x):
    a = x.astype(BF16)
    r = x - a.astype(F32)
    b = r.astype(BF16)
    c = (r - b.astype(F32)).astype(BF16)
    return a, b, c


def _rope(a, c_tab, s1_tab, s2_tab):
    w = a.shape[-1]
    reps = w // LANE
    ct = jnp.concatenate([c_tab] * reps, axis=1) if reps > 1 else c_tab
    s1 = jnp.concatenate([s1_tab] * reps, axis=1) if reps > 1 else s1_tab
    s2 = jnp.concatenate([s2_tab] * reps, axis=1) if reps > 1 else s2_tab
    half = ROT_DIM // 2
    return a * ct + pltpu.roll(a, w - half, axis=1) * s1 + pltpu.roll(a, half, axis=1) * s2


def _const_spec(shape, single=True):
    nd = len(shape)
    kw = {"pipeline_mode": pl.Buffered(1)} if single else {}
    return pl.BlockSpec(tuple(shape), lambda *a, _nd=nd: (0,) * _nd, **kw)


def _params(sem):
    return pltpu.CompilerParams(dimension_semantics=sem, vmem_limit_bytes=VMEM_LIMIT)


def _layer_a_kernel(x_ref, g_ref, win_ref, lng_ref, lnb_ref, ws_ref, bs_ref, wout_ref, *rest, sample, tm, aw):
    if sample:
        xo_ref, v_ref = rest
    else:
        xo_ref, mix_ref = rest
    x = x_ref[...]
    xn = _rms(x, g_ref[...]).astype(BF16)
    v = _gelu(_dot(xn, win_ref[:, aw:2 * aw]))
    vc = v - jnp.mean(v, axis=-1, keepdims=True)
    v = vc * lax.rsqrt(jnp.mean(vc * vc, axis=-1, keepdims=True) + EPS) * lng_ref[...] + lnb_ref[...]
    if sample:
        v_ref[...] = v
        mix = v * ws_ref[...] + bs_ref[...]
    else:
        vb = v.astype(BF16)
        tri = (lax.broadcasted_iota(jnp.int32, (A_CHUNK, A_CHUNK), 0)
               >= lax.broadcasted_iota(jnp.int32, (A_CHUNK, A_CHUNK), 1))
        for g in range(A_GROUPS):
            wm = jnp.where(tri, ws_ref[g], 0.0).astype(BF16)
            bias = bs_ref[:, g:g + 1]
            for c in range(tm // A_CHUNK):
                blk = vb[c * A_CHUNK:(c + 1) * A_CHUNK, g * A_CHUNK:(g + 1) * A_CHUNK]
                mix_ref[c * A_CHUNK:(c + 1) * A_CHUNK, g * A_CHUNK:(g + 1) * A_CHUNK] = _dot(wm, blk) + bias
        mix = mix_ref[...]
    u = _gelu(_dot(xn, win_ref[:, 0:aw]))
    z = _dot(xn, win_ref[:, 2 * aw:3 * aw])
    y = (u * mix * _silu(z)).astype(BF16)
    xo_ref[...] = _dot(y, wout_ref[...]) + x


def _layer_a(x2d, norm_g, w_in, ln_g, ln_b, w_s, b_s, w_out, *, sample):
    r, d = x2d.shape
    aw = w_out.shape[0]
    tm = r if sample else TM_ROWS
    assert r % tm == 0 and tm % A_CHUNK == 0 or sample
    if sample:
        ws_arg = jnp.repeat(w_s[:, 0, 0], A_CHUNK)[None, :]
        bs_arg = jnp.repeat(b_s[:, 0], A_CHUNK)[None, :]
    else:
        ws_arg = w_s
        bs_arg = b_s.T
    row = pl.BlockSpec((tm, d), lambda i: (i, 0))
    in_specs = [row, _const_spec((1, d)), _const_spec(w_in.shape), _const_spec((1, aw)), _const_spec((1, aw)),
                _const_spec(ws_arg.shape), _const_spec(bs_arg.shape), _const_spec(w_out.shape)]
    if sample:
        out_shape = (jax.ShapeDtypeStruct((r, d), F32), jax.ShapeDtypeStruct((r, aw), F32))
        out_specs = (row, pl.BlockSpec((tm, aw), lambda i: (i, 0)))
        scratch = []
    else:
        out_shape = jax.ShapeDtypeStruct((r, d), F32)
        out_specs = row
        scratch = [pltpu.VMEM((tm, aw), F32)]
    return pl.pallas_call(
        functools.partial(_layer_a_kernel, sample=sample, tm=tm, aw=aw),
        grid=(r // tm,), in_specs=in_specs, out_specs=out_specs, out_shape=out_shape,
        scratch_shapes=scratch, compiler_params=_params(("parallel",)),
        name="layer_a_sample" if sample else "layer_a_prompt",
    )(x2d, norm_g[None, :], w_in, ln_g[None, :], ln_b[None, :], ws_arg, bs_arg, w_out)


def _proj_kernel(x_ref, g_ref, wq_ref, wkv_ref, *rest, prompt, tm, seq):
    if prompt:
        (wvp_ref, cq_ref, s1q_ref, s2q_ref, ck_ref, s1k_ref, s2k_ref,
         qc_ref, qr_ref, kvc_ref, kvs_ref, kvw_ref, ksa_ref, kwa_ref, vsp_ref, vwp_ref) = rest
    else:
        (cq_ref, s1q_ref, s2q_ref, ck_ref, s1k_ref, s2k_ref,
         qc_ref, qr_ref, kvc_ref, kvs_ref, kvw_ref) = rest
    scale = HEAD_DIM ** -0.5
    xn = _rms(x_ref[...], g_ref[...]).astype(BF16)
    cq, s1q, s2q = cq_ref[...], s1q_ref[...], s2q_ref[...]
    ck, s1k, s2k = ck_ref[...], s1k_ref[...], s2k_ref[...]
    q = _dot(xn, wq_ref[...])
    qc_ref[...] = (q * scale).astype(BF16)
    qr_ref[...] = (_rope(q, cq, s1q, s2q) * scale).astype(BF16)
    kv = _dot(xn, wkv_ref[...])
    kvc = kv[:, 0:KVROW]
    kvs = jnp.concatenate([_rope(kv[:, KVROW:KVROW + KVW], ck, s1k, s2k), kv[:, KVROW + KVW:2 * KVROW]], axis=1)
    kvw = jnp.concatenate([_rope(kv[:, 2 * KVROW:2 * KVROW + KVW], ck, s1k, s2k),
                           kv[:, 2 * KVROW + KVW:3 * KVROW]], axis=1)
    if not prompt:
        kvc_ref[...] = kvc
        kvs_ref[...] = kvs
        kvw_ref[...] = kvw
        return
    kvc_ref[...] = jnp.transpose(kvc)
    kvs_t = jnp.transpose(kvs)
    kvw_t = jnp.transpose(kvw)
    kvs_ref[...] = kvs_t
    kvw_ref[...] = kvw_t
    t0 = (pl.program_id(0) % (seq // tm)) * tm
    key_blk = (t0 + lax.broadcasted_iota(jnp.int32, (HEAD_DIM, tm), 1)) // SEL_BLOCK
    onehot_t = jnp.where(key_blk == lax.broadcasted_iota(jnp.int32, (HEAD_DIM, tm), 0), 1.0, 0.0)
    zeros_t = jnp.zeros((SLOT - HEAD_DIM, tm), F32)
    for g in range(N_KV):
        ksa_ref[g * SLOT:(g + 1) * SLOT, :] = jnp.concatenate(
            [kvs_t[g * HEAD_DIM:(g + 1) * HEAD_DIM], onehot_t], axis=0).astype(BF16)
        kwa_ref[g * SLOT:(g + 1) * SLOT, :] = jnp.concatenate(
            [kvw_t[g * HEAD_DIM:(g + 1) * HEAD_DIM], zeros_t], axis=0).astype(BF16)
    pw = N_KV * SLOT
    vp = _dot(xn, wvp_ref[...])
    vsp_ref[...] = vp[:, 0:pw].astype(BF16)
    vwp_ref[...] = vp[:, pw:2 * pw].astype(BF16)


def _project(x2d, norm_g, wq_pad, wkv, wv_pad, tabs_q, tabs_k, *, prompt, seq):
    r, d = x2d.shape
    tm = TQ if prompt else r
    nt = seq // tm if prompt else 1
    row = lambda w: pl.BlockSpec((tm, w), lambda i: (i, 0))
    tab = pl.BlockSpec((tm, LANE), (lambda i: (i % nt, 0)) if prompt else (lambda i: (0, 0)))
    qw = N_HEADS * SLOT
    pw = N_KV * SLOT
    in_specs = [row(d), _const_spec((1, d)), _const_spec(wq_pad.shape), _const_spec(wkv.shape)]
    args = [x2d, norm_g[None, :], wq_pad, wkv]
    if prompt:
        in_specs.append(_const_spec(wv_pad.shape))
        args.append(wv_pad)
    in_specs += [tab] * 6
    args += list(tabs_q) + list(tabs_k)
    out_shape = [jax.ShapeDtypeStruct((r, qw), BF16), jax.ShapeDtypeStruct((r, qw), BF16)]
    out_specs = [row(qw), row(qw)]
    if prompt:
        nbatch = r // seq
        out_shape += [jax.ShapeDtypeStruct((nbatch, KVROW, seq), F32)] * 3
        out_specs += [pl.BlockSpec((None, KVROW, tm), lambda i: (i // nt, 0, i % nt))] * 3
        out_shape += [jax.ShapeDtypeStruct((nbatch, nt, pw, tm), BF16)] * 2
        out_specs += [pl.BlockSpec((None, None, pw, tm), lambda i: (i // nt, i % nt, 0, 0))] * 2
        out_shape += [jax.ShapeDtypeStruct((r, pw), BF16)] * 2
        out_specs += [row(pw)] * 2
    else:
        out_shape += [jax.ShapeDtypeStruct((r, KVROW), F32)] * 3
        out_specs += [row(KVROW)] * 3
    return pl.pallas_call(
        functools.partial(_proj_kernel, prompt=prompt, tm=tm, seq=seq),
        grid=(r // tm,), in_specs=in_specs, out_specs=tuple(out_specs), out_shape=tuple(out_shape),
        compiler_params=_params(("parallel",)),
        name="nsa_project_prompt" if prompt else "nsa_project_sample",
    )(*args)


def _cmp_lh_kernel(*refs, n_in, n_scalar):
    refs = refs[n_scalar:]
    x_refs = refs[:n_in]
    w_ref, lh_ref, xs_ref = refs[n_in], refs[n_in + 1], refs[n_in + 2]
    half = w_ref.shape[-1]
    tpb = x_refs[0].shape[-1]
    n_rows = lh_ref.shape[0]
    for kv in range(2):
        for ri, r in enumerate(x_refs):
            for t in range(tpb // LANE):
                for gp in range(N_KV // 2):
                    tile = jnp.concatenate([r[kv, 2 * gp, :, t * LANE:(t + 1) * LANE],
                                            r[kv, 2 * gp + 1, :, t * LANE:(t + 1) * LANE]], axis=0)
                    row0 = ri * tpb + t * LANE
                    xs_ref[gp, row0:row0 + LANE, :] = jnp.transpose(tile)
        acc = None
        for j in range(CMP_STRIDE):
            xj = jnp.concatenate([xs_ref[gp, pl.ds(j, n_rows, stride=CMP_STRIDE), :]
                                  for gp in range(N_KV // 2)], axis=1).astype(BF16)
            t = _dot(xj, w_ref[2 * j + kv])
            acc = t if acc is None else acc + t
        lh_ref[:, kv * half:(kv + 1) * half] = acc


def _cmp_lh_prompt(kvc_t, wc):
    nbatch, _, _, _, seq = kvc_t.shape
    tt = min(seq, 2048)
    assert seq % tt == 0 and tt % LANE == 0
    nt = seq // tt
    out_w = 2 * wc.shape[-1]
    return pl.pallas_call(
        functools.partial(_cmp_lh_kernel, n_in=1, n_scalar=0),
        grid=(nbatch, nt),
        in_specs=[pl.BlockSpec((None, 2, N_KV, HEAD_DIM, tt), lambda b, i: (b, 0, 0, 0, i)), _const_spec(wc.shape)],
        out_specs=pl.BlockSpec((tt // CMP_STRIDE, out_w), lambda b, i: (b * nt + i, 0)),
        out_shape=jax.ShapeDtypeStruct((nbatch * seq // CMP_STRIDE, out_w), F32),
        scratch_shapes=[pltpu.VMEM((N_KV // 2, tt, LANE), F32)],
        compiler_params=_params(("parallel", "parallel")), name="cmp_lh_prompt",
    )(kvc_t, wc)


def _cmp_lh_sample(pool_t, page_flat, wc):
    n_pages = page_flat.shape[0]
    pps = PAGES_PER_STEP
    assert n_pages % pps == 0
    steps = n_pages // pps
    blk = (None,) + tuple(pool_t.shape[1:])
    page_specs = [pl.BlockSpec(blk, (lambda i, pt, _p=p: (pt[i * pps + _p], 0, 0, 0, 0))) for p in range(pps)]
    nd = wc.ndim
    w_spec = pl.BlockSpec(wc.shape, lambda i, pt: (0,) * nd, pipeline_mode=pl.Buffered(1))
    out_w = 2 * wc.shape[-1]
    rows = pps * PAGE_SIZE // CMP_STRIDE
    grid_spec = pltpu.PrefetchScalarGridSpec(
        num_scalar_prefetch=1, grid=(steps,),
        in_specs=page_specs + [w_spec],
        out_specs=pl.BlockSpec((rows, out_w), lambda i, pt: (i, 0)),
        scratch_shapes=[pltpu.VMEM((N_KV // 2, pps * PAGE_SIZE, LANE), F32)])
    return pl.pallas_call(
        functools.partial(_cmp_lh_kernel, n_in=pps, n_scalar=1),
        grid_spec=grid_spec,
        out_shape=jax.ShapeDtypeStruct((steps * rows, out_w), F32),
        compiler_params=_params(("parallel",)), name="cmp_lh_sample",
    )(page_flat, *([pool_t] * pps), wc)


def _compress_finish(lh, pe_ref, w1f_ref, b1_ref, w2_ref, b2_ref):
    nc = lh.shape[0]
    outs = []
    for kv in range(2):
        pe8 = jnp.broadcast_to(pe_ref[kv], (8, pe_ref.shape[-1])).astype(BF16)
        cb = _dot(pe8, w1f_ref[kv])[0:1, :] + b1_ref[kv]
        lo = lh[:, kv * 2 * KVW:kv * 2 * KVW + KVW]
        hi = lh[:, kv * 2 * KVW + KVW:(kv + 1) * 2 * KVW]
        hi_next = pltpu.roll(hi, nc - 1, axis=0)
        hid = _silu(lo + hi_next + cb)
        outs.append((_dot(hid.astype(BF16), w2_ref[kv]) + b2_ref[kv]).astype(BF16))
    return outs


def _select_blocks(imp, blk, jq, axis):
    forced = (blk == 0) | (blk == jq) | (blk == jq - 1)
    score = jnp.where((blk > jq) | (blk < 0), NEG_INF, jnp.where(forced, jnp.inf, imp))
    blk_f = blk.astype(F32)
    sel = jnp.zeros(imp.shape, F32)
    idxs = []
    for _ in range(N_SELECT):
        m = jnp.max(score, axis=axis, keepdims=True)
        idx = jnp.min(jnp.where(score == m, blk_f, 1e9), axis=axis, keepdims=True)
        hit = (blk_f == idx) & (m > NEG_INF)
        sel = jnp.where(hit, 1.0, sel)
        score = jnp.where(blk_f == idx, NEG_INF, score)
        idxs.append(idx)
    return sel, idxs


def _gate(xn, wz_ref, wgl_ref, gb_ref, e_ref):
    z = _dot(xn, wz_ref[...])
    gl = _dot(xn, wgl_ref[...]) + gb_ref[...]
    e = e_ref[...]
    a, b, c = _split3(gl)
    gle = _dot(a, e) + _dot(b, e) + _dot(c, e)
    return _silu(z), jax.nn.sigmoid(gle)


def _gate_kernel(x_ref, g_ref, wz_ref, wgl_ref, gb_ref, e_ref, sz_ref, gg_ref):
    xn = _rms(x_ref[...], g_ref[...]).astype(BF16)
    sz, gg = _gate(xn, wz_ref, wgl_ref, gb_ref, e_ref)
    sz_ref[...] = sz
    gg_ref[...] = gg


def _out_kernel(y_ref, x_ref, w_ref, g_ref, o_ref):
    o_ref[...] = _rms(_dot(y_ref[...].astype(BF16), w_ref[...]) + x_ref[...], g_ref[...])


def _attn_prompt_kernel(x_ref, qc_ref, qr_ref, lh_ref, ksa_ref, vsp_ref,
                        kw0_ref, kw1_ref, kw2_ref, vw0_ref, vw1_ref, vw2_ref,
                        g1_ref, gf_ref, wz_ref, wgl_ref, gb_ref, e_ref, wout_ref,
                        pe_ref, w1f_ref, b1_ref, w2_ref, b2_ref, ovt_ref,
                        out_ref, kc_scr, vc_scr, o_scr, *, tq, seq):
    qi = pl.program_id(1)
    t0 = qi * tq
    nc = seq // CMP_STRIDE
    rows = Q_PER_KV * tq

    @pl.when(qi == 0)
    def _():
        kc, vc = _compress_finish(lh_ref[...], pe_ref, w1f_ref, b1_ref, w2_ref, b2_ref)
        kc_scr[...] = kc
        vc_scr[...] = vc

    a_row = lax.broadcasted_iota(jnp.int32, (rows, 1), 0) % tq
    pos_row = t0 + a_row
    c_col = lax.broadcasted_iota(jnp.int32, (1, tq), 1)
    n_col = lax.broadcasted_iota(jnp.int32, (1, nc), 1)
    mask_c = (n_col * CMP_STRIDE + CMP_BLOCK - 1) <= pos_row
    causal = c_col <= a_row
    blk_t = lax.broadcasted_iota(jnp.int32, (2 * SEL_BLOCK, tq), 0) - SEL_BLOCK
    jq_t = (t0 + lax.broadcasted_iota(jnp.int32, (2 * SEL_BLOCK, tq), 1)) // SEL_BLOCK
    lane_q = lax.broadcasted_iota(jnp.int32, (tq, SLOT), 1)
    tiny = jnp.finfo(jnp.float32).tiny

    def stack(ref, g):
        return jnp.concatenate([ref[:, (Q_PER_KV * g + r) * SLOT:(Q_PER_KV * g + r + 1) * SLOT]
                                for r in range(Q_PER_KV)], axis=0)

    def put(br, g, o):
        for j in range(Q_PER_KV // 2):
            pair = jnp.concatenate([o[(2 * j) * tq:(2 * j + 1) * tq, 0:HEAD_DIM],
                                    o[(2 * j + 1) * tq:(2 * j + 2) * tq, 0:HEAD_DIM]], axis=1)
            h0 = Q_PER_KV * g + 2 * j
            o_scr[br, :, h0 * HEAD_DIM:(h0 + 2) * HEAD_DIM] = pair

    for g in range(N_KV):
        gs = slice(g * SLOT, (g + 1) * SLOT)
        s = _dot_nt(stack(qc_ref, g), kc_scr[:, gs])
        s = jnp.where(mask_c, s, NEG_INF)
        m = jnp.max(s, axis=-1, keepdims=True)
        m = jnp.where(m > NEG_INF, m, 0.0)
        e = jnp.where(mask_c, jnp.exp(s - m), 0.0)
        p = e / jnp.maximum(jnp.sum(e, axis=-1, keepdims=True), tiny)
        put(0, g, _dot(p.astype(BF16), vc_scr[:, gs]))
        psum = p[0:tq] + p[tq:2 * tq] + p[2 * tq:3 * tq] + p[3 * tq:4 * tq]
        ovt = ovt_ref[...]
        pa, pb, pc = _split3(psum)
        imp_t = _dot_nt(ovt, pa) + _dot_nt(ovt, pb) + _dot_nt(ovt, pc)
        sel_t, _ = _select_blocks(imp_t, blk_t, jq_t, axis=0)
        sel = jnp.transpose(sel_t)
        maskpart = jnp.where(lane_q >= HEAD_DIM, (sel - 1.0) * MASK_BIG, 0.0).astype(BF16)
        qr_g = stack(qr_ref, g)
        qa = jnp.where(jnp.concatenate([lane_q] * Q_PER_KV, axis=0) < HEAD_DIM, qr_g,
                       jnp.concatenate([maskpart] * Q_PER_KV, axis=0))

        def sel_tile(kt, carry, diag):
            m_i, l_i, acc = carry
            start = pl.multiple_of(kt * tq, tq)
            k_t = ksa_ref[kt, g * SLOT:(g + 1) * SLOT, :]
            v = vsp_ref[pl.ds(start, tq), gs]
            st = _dot(qa, k_t)
            if diag:
                st = jnp.where(causal, st, -MASK_BIG)
            m_new = jnp.maximum(m_i, jnp.max(st, axis=-1, keepdims=True))
            alpha = jnp.exp(m_i - m_new)
            pt = jnp.exp(st - m_new)
            l_new = alpha * l_i + jnp.sum(pt, axis=-1, keepdims=True)
            acc_new = alpha * acc + _dot(pt.astype(BF16), v)
            return m_new, l_new, acc_new

        init = (jnp.full((rows, 1), NEG_INF, F32), jnp.zeros((rows, 1), F32), jnp.zeros((rows, SLOT), F32))
        carry = lax.fori_loop(0, qi, lambda kt, c: sel_tile(kt, c, False), init)
        _, l_s, acc_s = sel_tile(qi, carry, True)
        put(1, g, acc_s / l_s)
        s0 = _dot(qr_g, kw0_ref[gs, :])
        s1 = _dot(qr_g, kw1_ref[gs, :])
        s2 = _dot(qr_g, kw2_ref[gs, :])
        s0 = jnp.where((c_col > a_row) & (qi >= 2), s0, NEG_INF)
        s1 = jnp.where(qi >= 1, s1, NEG_INF)
        s2 = jnp.where(causal, s2, NEG_INF)
        mw = jnp.maximum(jnp.maximum(jnp.max(s0, axis=-1, keepdims=True), jnp.max(s1, axis=-1, keepdims=True)),
                         jnp.max(s2, axis=-1, keepdims=True))
        p0, p1, p2 = jnp.exp(s0 - mw), jnp.exp(s1 - mw), jnp.exp(s2 - mw)
        lw = (jnp.sum(p0, axis=-1, keepdims=True) + jnp.sum(p1, axis=-1, keepdims=True)
              + jnp.sum(p2, axis=-1, keepdims=True))
        ow = (_dot(p0.astype(BF16), vw0_ref[:, gs]) + _dot(p1.astype(BF16), vw1_ref[:, gs])
              + _dot(p2.astype(BF16), vw2_ref[:, gs]))
        put(2, g, ow / lw)

    x = x_ref[...]
    xn = _rms(x, g1_ref[...]).astype(BF16)
    sz, gg = _gate(xn, wz_ref, wgl_ref, gb_ref, e_ref)
    qw = N_HEADS * HEAD_DIM
    y = None
    for br in range(N_BRANCH):
        t = o_scr[br] * sz[:, br * qw:(br + 1) * qw] * gg[:, br * qw:(br + 1) * qw]
        y = t if y is None else y + t
    out_ref[...] = _rms(_dot(y.astype(BF16), wout_ref[...]) + x, gf_ref[...])


def _attn_prompt(x1, qc, qr, lh, ksa, vsp, kwp, vwp, g1, gf, wz, wgl, gb, e_mat, wout,
                 pe, w1f, b1, w2p, b2p, ovt, *, batch, seq):
    tq = TQ
    assert seq % tq == 0 and WINDOW == 2 * tq and seq % SEL_BLOCK == 0 and seq // SEL_BLOCK <= SEL_BLOCK
    nq = seq // tq
    nc = seq // CMP_STRIDE
    d = x1.shape[1]
    qw = N_HEADS * SLOT
    pw = N_KV * SLOT
    tile = lambda w: pl.BlockSpec((tq, w), lambda b, i: (b * nq + i, 0))
    per_b = lambda rows, w: pl.BlockSpec((rows, w), lambda b, i: (b, 0), pipeline_mode=pl.Buffered(1))
    ksa_spec = pl.BlockSpec((None, nq, pw, tq), lambda b, i: (b, 0, 0, 0), pipeline_mode=pl.Buffered(1))
    kwin = lambda off: pl.BlockSpec((None, None, pw, tq), lambda b, i: (b, jnp.maximum(i - off, 0), 0, 0))
    win = lambda off: pl.BlockSpec((tq, pw), lambda b, i: (b * nq + jnp.maximum(i - off, 0), 0))
    consts = [g1[None, :], gf[None, :], wz, wgl, gb, e_mat, wout, pe, w1f, b1, w2p, b2p, ovt]
    in_specs = ([tile(d), tile(qw), tile(qw), per_b(nc, lh.shape[1]), ksa_spec, per_b(seq, pw),
                 kwin(2), kwin(1), kwin(0), win(2), win(1), win(0)]
                + [_const_spec(c.shape) for c in consts])
    return pl.pallas_call(
        functools.partial(_attn_prompt_kernel, tq=tq, seq=seq),
        grid=(batch, nq), in_specs=in_specs, out_specs=tile(d),
        out_shape=jax.ShapeDtypeStruct((batch * seq, d), F32),
        scratch_shapes=[pltpu.VMEM((nc, pw), BF16), pltpu.VMEM((nc, pw), BF16),
                        pltpu.VMEM((N_BRANCH, tq, N_HEADS * HEAD_DIM), F32)],
        compiler_params=_params(("arbitrary", "arbitrary")), name="nsa_attend_prompt",
    )(x1, qc, qr, lh, ksa, vsp, kwp, kwp, kwp, vwp, vwp, vwp, *consts)


def _sample_cmp_kernel(lh_ref, qc_ref, pe_ref, w1f_ref, b1_ref, w2_ref, b2_ref, ov_ref,
                       topi_ref, oc_ref, *, past):
    kc, vc = _compress_finish(lh_ref[...], pe_ref, w1f_ref, b1_ref, w2_ref, b2_ref)
    nc = kc.shape[0]
    nbp = ov_ref.shape[1]
    q = qc_ref[...]
    head_row = lax.broadcasted_iota(jnp.int32, (N_HEADS, 1), 0)
    n_col = lax.broadcasted_iota(jnp.int32, (1, nc), 1)
    mask_c = (n_col * CMP_STRIDE + CMP_BLOCK - 1) <= past
    row8 = lax.broadcasted_iota(jnp.int32, (8, 1), 0)
    tiny = jnp.finfo(jnp.float32).tiny
    oc = jnp.zeros((N_HEADS, SLOT), F32)
    psum8 = jnp.zeros((8, nc), F32)
    for g in range(N_KV):
        gs = slice(g * SLOT, (g + 1) * SLOT)
        in_g = (head_row // Q_PER_KV) == g
        s = jnp.where(mask_c, _dot_nt(q, kc[:, gs]), NEG_INF)
        m = jnp.max(s, axis=-1, keepdims=True)
        m = jnp.where(m > NEG_INF, m, 0.0)
        e = jnp.where(mask_c, jnp.exp(s - m), 0.0)
        p = e / jnp.maximum(jnp.sum(e, axis=-1, keepdims=True), tiny)
        oc = jnp.where(in_g, _dot(p.astype(BF16), vc[:, gs]), oc)
        pg = jnp.sum(jnp.where(in_g, p, 0.0), axis=0, keepdims=True)
        psum8 = jnp.where(row8 == g, pg, psum8)
    ov = ov_ref[...]
    pa, pb, pc = _split3(psum8)
    imp = _dot(pa, ov) + _dot(pb, ov) + _dot(pc, ov)
    blk = lax.broadcasted_iota(jnp.int32, (8, nbp), 1)
    n_blk = past // SEL_BLOCK + 1
    blk = jnp.where(blk < n_blk, blk, -1)
    _, idxs = _select_blocks(imp, blk, jnp.int32(past // SEL_BLOCK), axis=1)
    lane = lax.broadcasted_iota(jnp.int32, (8, LANE), 1)
    topi = jnp.zeros((8, LANE), jnp.int32)
    for r, idx in enumerate(idxs):
        topi = jnp.where(lane == r, idx.astype(jnp.int32), topi)
    topi_ref[...] = topi
    oc_ref[...] = oc


def _sample_cmp(lh3, qc3, pe, w1f, b1, w2p, b2p, ov, *, past):
    nb, nc, w = lh3.shape
    consts = [pe, w1f, b1, w2p, b2p, ov]
    return pl.pallas_call(
        functools.partial(_sample_cmp_kernel, past=past),
        grid=(nb,),
        in_specs=[pl.BlockSpec((None, nc, w), lambda b: (b, 0, 0)),
                  pl.BlockSpec((None, N_HEADS, SLOT), lambda b: (b, 0, 0))]
                 + [_const_spec(c.shape) for c in consts],
        out_specs=(pl.BlockSpec((None, 8, LANE), lambda b: (b, 0, 0)),
                   pl.BlockSpec((None, N_HEADS, SLOT), lambda b: (b, 0, 0))),
        out_shape=(jax.ShapeDtypeStruct((nb, 8, LANE), jnp.int32),
                   jax.ShapeDtypeStruct((nb, N_HEADS, SLOT), F32)),
        compiler_params=_params(("parallel",)), name="nsa_sample_cmp_topk",
    )(lh3, qc3, *consts)


def _sample_attn_kernel(pt_ref, ti_ref, *refs, n_pool_blk, wlen):
    k_refs = refs[:N_SELECT]
    v_refs = refs[N_SELECT:2 * N_SELECT]
    (qr_ref, ksn_ref, vsn_ref, kwn_ref, vwn_ref, win_ref, wnew_ref, oc_ref, sz_ref, gg_ref,
     y_ref, wout_ref) = refs[2 * N_SELECT:]
    b = pl.program_id(0)
    g = pl.program_id(1)
    per_page = PAGE_SIZE // SEL_BLOCK
    q = qr_ref[...]
    qf = q[:, 0:HEAD_DIM].astype(F32)
    in_g = (lax.broadcasted_iota(jnp.int32, (N_HEADS, 1), 0) // Q_PER_KV) == g
    zpad = jnp.zeros((SLOT - HEAD_DIM, PAGE_SIZE), BF16)
    tok_half = lax.broadcasted_iota(jnp.int32, (1, PAGE_SIZE), 1) // SEL_BLOCK

    s_new = jnp.sum(qf * ksn_ref[...], axis=-1, keepdims=True)
    scores = []
    m = s_new
    for r in range(N_SELECT):
        t = ti_ref[(b * N_KV + g) * N_SELECT + r]
        ok = (tok_half == t % per_page) & (t < n_pool_blk)
        k_t = jnp.concatenate([k_refs[r][...].astype(BF16), zpad], axis=0)
        s = jnp.where(ok, _dot(q, k_t), NEG_INF)
        scores.append(s)
        m = jnp.maximum(m, jnp.max(s, axis=-1, keepdims=True))
    p_new = jnp.exp(s_new - m)
    l = p_new
    o = p_new * vsn_ref[...]
    for r in range(N_SELECT):
        p = jnp.exp(scores[r] - m)
        l = l + jnp.sum(p, axis=-1, keepdims=True)
        o = o + _dot_nt(p.astype(BF16), v_refs[r][...].astype(BF16))
    o_s = o / l

    wkey = lax.broadcasted_iota(jnp.int32, (1, wlen), 1)
    kw_t = jnp.concatenate([win_ref[0].astype(BF16), jnp.zeros((SLOT - HEAD_DIM, wlen), BF16)], axis=0)
    s = jnp.where(wkey > wlen - WINDOW, _dot(q, kw_t), NEG_INF)
    s_new = jnp.sum(qf * kwn_ref[...], axis=-1, keepdims=True)
    m = jnp.maximum(jnp.max(s, axis=-1, keepdims=True), s_new)
    p = jnp.exp(s - m)
    p_new = jnp.exp(s_new - m)
    l = jnp.sum(p, axis=-1, keepdims=True) + p_new
    o_w = (_dot_nt(p.astype(BF16), win_ref[1].astype(BF16)) + p_new * vwn_ref[...]) / l

    o_c = oc_ref[:, 0:HEAD_DIM]
    y = o_c * sz_ref[0] * gg_ref[0] + o_s * sz_ref[1] * gg_ref[1] + o_w * sz_ref[2] * gg_ref[2]

    @pl.when(g == 0)
    def _():
        y_ref[...] = jnp.zeros(y_ref.shape, F32)

    y_ref[...] = jnp.where(in_g, y, y_ref[...])
    for kv in range(2):
        shifted = pltpu.roll(win_ref[kv], wlen - 1, axis=1)
        wout_ref[kv] = jnp.where(wkey == wlen - 1, wnew_ref[kv], shifted)


def _sample_attn(page_flat, topi_flat, pool_t, qr3, ksn, vsn, kwn, vwn, win_t, wnew, oc, sz4, gg4, *, n_pages):
    nb = qr3.shape[0]
    wlen = win_t.shape[-1]
    per_page = PAGE_SIZE // SEL_BLOCK
    n_pool_blk = n_pages * per_page

    def pool_map(r, kv):
        def f(b, g, pt, ti):
            blk = jnp.minimum(ti[(b * N_KV + g) * N_SELECT + r], n_pool_blk - 1)
            return (pt[b * n_pages + blk // per_page], kv, g, 0, 0)
        return f

    b3 = lambda s1, s2: pl.BlockSpec((None, s1, s2), lambda b, g, pt, ti: (b, 0, 0))
    b4 = pl.BlockSpec((None, N_BRANCH, N_HEADS, HEAD_DIM), lambda b, g, pt, ti: (b, 0, 0, 0))
    page = lambda r, kv: pl.BlockSpec((None, None, None, HEAD_DIM, PAGE_SIZE), pool_map(r, kv))
    wspec = lambda last: pl.BlockSpec((None, 2, None, HEAD_DIM, last), lambda b, g, pt, ti: (b, 0, g, 0, 0))
    in_specs = ([page(r, 0) for r in range(N_SELECT)] + [page(r, 1) for r in range(N_SELECT)]
                + [b3(N_HEADS, SLOT)] + [b3(N_HEADS, HEAD_DIM)] * 4
                + [wspec(wlen), wspec(1), b3(N_HEADS, SLOT), b4, b4])
    grid_spec = pltpu.PrefetchScalarGridSpec(
        num_scalar_prefetch=2, grid=(nb, N_KV), in_specs=in_specs,
        out_specs=(b3(N_HEADS, HEAD_DIM), wspec(wlen)))
    return pl.pallas_call(
        functools.partial(_sample_attn_kernel, n_pool_blk=n_pool_blk, wlen=wlen),
        grid_spec=grid_spec,
        out_shape=(jax.ShapeDtypeStruct((nb, N_HEADS, HEAD_DIM), F32),
                   jax.ShapeDtypeStruct(win_t.shape, F32)),
        compiler_params=_params(("arbitrary", "arbitrary")), name="nsa_sample_sel_win",
    )(page_flat, topi_flat, *([pool_t] * (2 * N_SELECT)), qr3, ksn, vsn, kwn, vwn, win_t, wnew, oc, sz4, gg4)


def _rope_tables(pos):
    half = ROT_DIM // 2
    freqs = jnp.exp(-math.log(ROPE_THETA) * jnp.arange(half, dtype=F32) * (2.0 / ROT_DIM))
    ang = pos.astype(F32)[:, None] * freqs[None, :]
    cos, sin = jnp.cos(ang), jnp.sin(ang)
    n = pos.shape[0]
    one = jnp.ones((n, HEAD_DIM - ROT_DIM), F32)
    zero8 = jnp.zeros((n, half), F32)
    zrest = jnp.zeros((n, HEAD_DIM - ROT_DIM), F32)
    c64 = jnp.concatenate([cos, cos, one], axis=1)
    s1_64 = jnp.concatenate([-sin, zero8, zrest], axis=1)
    s2_64 = jnp.concatenate([zero8, sin, zrest], axis=1)
    pad1 = jnp.ones((n, SLOT - HEAD_DIM), F32)
    pad0 = jnp.zeros((n, SLOT - HEAD_DIM), F32)
    tabs_q = (jnp.concatenate([c64, pad1], axis=1), jnp.concatenate([s1_64, pad0], axis=1),
              jnp.concatenate([s2_64, pad0], axis=1))
    tabs_k = tuple(jnp.concatenate([t, t], axis=1) for t in (c64, s1_64, s2_64))
    return tabs_q, tabs_k


def _pad_heads(w, n_heads):
    k = w.shape[0]
    w3 = w.reshape(k, n_heads, HEAD_DIM)
    return jnp.pad(w3, ((0, 0), (0, 0), (0, SLOT - HEAD_DIM))).reshape(k, n_heads * SLOT)


def _overlap(nc, nb):
    n = np.arange(nc)[:, None] * CMP_STRIDE
    j = np.arange(nb)[None, :] * SEL_BLOCK
    return ((n <= j + SEL_BLOCK - 1) & (n + CMP_BLOCK - 1 >= j)).astype(np.float32)


def kernel(x_prompt, x_sample, cache_cmp_kv, cache_sel_kv, state_win_kv, page_table, norm_g, final_norm_g,
           a_w_in, a_ln_g, a_ln_b, a_w_s, a_b_s, a_w_out, b_w_in, b_cmp_pe, b_cmp_w1, b_cmp_b1, b_cmp_w2,
           b_cmp_b2, b_gate_b, b_w_out):
    batch, seq, d = x_prompt.shape
    nb, dec_seq, _ = x_sample.shape
    assert dec_seq == 1
    n_pages = page_table.shape[1]
    past = n_pages * PAGE_SIZE
    assert past % SEL_BLOCK == 0 and past % CMP_STRIDE == 0 and past // SEL_BLOCK + 1 >= N_SELECT
    wlen = state_win_kv.shape[2]
    qw = N_HEADS * HEAD_DIM

    a_win = a_w_in[0].astype(BF16)
    a_wout = a_w_out[0].astype(BF16)
    xp1 = _layer_a(x_prompt.reshape(batch * seq, d), norm_g[0], a_win, a_ln_g[0], a_ln_b[0], a_w_s[0], a_b_s[0],
                   a_wout, sample=False)
    xs1, chunk_v = _layer_a(x_sample.reshape(nb, d), norm_g[0], a_win, a_ln_g[0], a_ln_b[0], a_w_s[0], a_b_s[0],
                            a_wout, sample=True)

    w_in = b_w_in[0]
    o1 = qw
    o2 = o1 + N_BRANCH * KVROW
    o3 = o2 + N_BRANCH * qw
    wq_pad = _pad_heads(w_in[:, :o1], N_HEADS).astype(BF16)
    wkv = w_in[:, o1:o2].astype(BF16)
    wv_sel = w_in[:, o1 + KVROW + KVW:o1 + 2 * KVROW]
    wv_win = w_in[:, o1 + 2 * KVROW + KVW:o1 + 3 * KVROW]
    wv_pad = jnp.concatenate([_pad_heads(w, N_KV) for w in (wv_sel, wv_win)], axis=1).astype(BF16)
    wz = w_in[:, o2:o3].astype(BF16)
    n_gate = N_BRANCH * N_HEADS
    wgl = jnp.pad(w_in[:, o3:], ((0, 0), (0, LANE - n_gate))).astype(BF16)
    gb = jnp.pad(b_gate_b[0].reshape(1, n_gate), ((0, 0), (0, LANE - n_gate)))
    e_np = np.zeros((LANE, n_gate * HEAD_DIM), np.float32)
    for j in range(n_gate):
        e_np[j, j * HEAD_DIM:(j + 1) * HEAD_DIM] = 1.0
    e_mat = jnp.asarray(e_np, dtype=BF16)
    wout = b_w_out[0].astype(BF16)
    w1 = b_cmp_w1[0]
    eye = jnp.eye(N_KV, dtype=F32)
    bd = jnp.einsum("Gg,kjch->kjGcgh", eye, w1).reshape(2, CMP_BLOCK, KVW, KVW)
    wc = jnp.concatenate([bd[:, :CMP_STRIDE], bd[:, CMP_STRIDE:]], axis=-1)
    wc = jnp.transpose(wc, (1, 0, 2, 3)).reshape(2 * CMP_STRIDE, KVW, 2 * KVW).astype(BF16)
    pe = b_cmp_pe[0].reshape(2, 1, CMP_BLOCK * HEAD_DIM)
    w1f = jnp.tile(w1.reshape(2, CMP_BLOCK * HEAD_DIM, HEAD_DIM), (1, 1, N_KV)).astype(BF16)
    b1 = jnp.tile(b_cmp_b1[0], (1, N_KV)).reshape(2, 1, KVW)
    w2bd = jnp.einsum("Gg,kch->kGcgh", eye, b_cmp_w2[0])
    w2p = jnp.pad(w2bd, ((0, 0),) * 4 + ((0, SLOT - HEAD_DIM),)).reshape(2, KVW, N_KV * SLOT).astype(BF16)
    b2p = jnp.pad(jnp.broadcast_to(b_cmp_b2[0][:, None, :], (2, N_KV, HEAD_DIM)),
                  ((0, 0), (0, 0), (0, SLOT - HEAD_DIM))).reshape(2, 1, N_KV * SLOT)

    tabs_q, tabs_k = _rope_tables(jnp.arange(seq, dtype=jnp.int32))
    (qc, qr, kvc_t, kvs_t, kvw_t, ksa, kwa, vsp, vwp) = _project(xp1, norm_g[1], wq_pad, wkv, wv_pad, tabs_q, tabs_k,
                                                                prompt=True, seq=seq)
    fm = lambda t: t.reshape(t.shape[0], 2, N_KV, HEAD_DIM, t.shape[-1])
    lh_p = _cmp_lh_prompt(fm(kvc_t), wc)
    ovt_np = np.zeros((2 * SEL_BLOCK, seq // CMP_STRIDE), np.float32)
    ovt_np[SEL_BLOCK:SEL_BLOCK + seq // SEL_BLOCK] = _overlap(seq // CMP_STRIDE, seq // SEL_BLOCK).T
    y_prompt = _attn_prompt(xp1, qc, qr, lh_p, ksa, vsp, kwa, vwp, norm_g[1], final_norm_g, wz, wgl, gb, e_mat,
                            wout, pe, w1f, b1, w2p, b2p, jnp.asarray(ovt_np, dtype=BF16), batch=batch, seq=seq)

    pos_s = jnp.full((nb,), past, dtype=jnp.int32)
    tabs_qs, tabs_ks = _rope_tables(pos_s)
    qc_s, qr_s, kvc_s, kvs_s, kvw_s = _project(xs1, norm_g[1], wq_pad, wkv, None, tabs_qs, tabs_ks,
                                               prompt=False, seq=1)
    page_flat = page_table.reshape(-1).astype(jnp.int32)
    n_phys = cache_cmp_kv.shape[1]
    to_fm = lambda t: jnp.transpose(t, (0, 2, 3, 4, 1))
    lh_s = _cmp_lh_sample(to_fm(cache_cmp_kv[0]), page_flat, wc)
    nc_s = past // CMP_STRIDE
    nb_blk = past // SEL_BLOCK + 1
    nbp = -(-nb_blk // LANE) * LANE
    ov_np = np.zeros((nc_s, nbp), np.float32)
    ov_np[:, :nb_blk] = _overlap(nc_s, nb_blk)
    topi, oc_s = _sample_cmp(lh_s.reshape(nb, nc_s, lh_s.shape[1]), qc_s.reshape(nb, N_HEADS, SLOT),
                             pe, w1f, b1, w2p, b2p, jnp.asarray(ov_np, dtype=BF16), past=past)
    topi_flat = topi[:, :N_KV, :N_SELECT].reshape(-1)
    r_gate = xs1.shape[0]
    sz_s, gg_s = pl.pallas_call(
        _gate_kernel, grid=(1,),
        in_specs=[_const_spec(s, single=False) for s in
                  ((r_gate, d), (1, d), wz.shape, wgl.shape, gb.shape, e_mat.shape)],
        out_specs=(_const_spec((r_gate, N_BRANCH * qw), single=False),) * 2,
        out_shape=(jax.ShapeDtypeStruct((r_gate, N_BRANCH * qw), F32),) * 2,
        compiler_params=_params(("arbitrary",)), name="nsa_gate_sample",
    )(xs1, norm_g[1][None, :], wz, wgl, gb, e_mat)
    expand = lambda t: jnp.repeat(t.reshape(nb, N_KV, HEAD_DIM), Q_PER_KV, axis=1)
    y_heads, win_s_t = _sample_attn(
        page_flat, topi_flat, to_fm(cache_sel_kv[0]), qr_s.reshape(nb, N_HEADS, SLOT),
        expand(kvs_s[:, :KVW]), expand(kvs_s[:, KVW:]), expand(kvw_s[:, :KVW]), expand(kvw_s[:, KVW:]),
        to_fm(state_win_kv[0]), kvw_s.reshape(nb, 2, N_KV, HEAD_DIM, 1), oc_s,
        sz_s.reshape(nb, N_BRANCH, N_HEADS, HEAD_DIM), gg_s.reshape(nb, N_BRANCH, N_HEADS, HEAD_DIM),
        n_pages=n_pages)
    y_sample = pl.pallas_call(
        _out_kernel, grid=(1,),
        in_specs=[_const_spec(s, single=False) for s in ((nb, qw), (nb, d), wout.shape, (1, d))],
        out_specs=_const_spec((nb, d), single=False),
        out_shape=jax.ShapeDtypeStruct((nb, d), F32),
        compiler_params=_params(("arbitrary",)), name="nsa_out_sample",
    )(y_heads.reshape(nb, qw), xs1, wout, final_norm_g[None, :])

    kv6 = lambda t: t.reshape(1, nb, 1, 2, N_KV, HEAD_DIM)
    from_fm = lambda t: jnp.transpose(t, (0, 4, 1, 2, 3))[None]
    wl_p = min(WINDOW, seq)
    return (y_prompt.reshape(batch, seq, d), y_sample.reshape(nb, 1, d),
            from_fm(fm(kvc_t)), kv6(kvc_s), from_fm(fm(kvs_t)), kv6(kvs_s),
            from_fm(fm(kvw_t)[..., seq - wl_p:]), from_fm(win_s_t), chunk_v.reshape(1, nb, 1, -1))
```

```python
import functools
import math

import numpy as np
import jax
import jax.numpy as jnp
from jax import lax
from jax.experimental import pallas as pl
from jax.experimental.pallas import tpu as pltpu

F32 = jnp.float32
BF16 = jnp.bfloat16
EPS = 1e-6

A_CHUNK = 128
A_GROUPS = 16
N_HEADS = 16
HEAD_DIM = 64
N_KV = 4
Q_PER_KV = N_HEADS // N_KV
ROT_DIM = HEAD_DIM // 4
ROPE_THETA = 500000.0
CMP_BLOCK = 32
CMP_STRIDE = 16
SEL_BLOCK = 64
N_SELECT = 16
WINDOW = 512
N_BRANCH = 3
PAGE_SIZE = 128

LANE = 128
SLOT = LANE
KVW = N_KV * HEAD_DIM
KVROW = 2 * KVW
TM_ROWS = 256
TQ = 256
PAGES_PER_STEP = 32
VMEM_LIMIT = 56 * 1024 * 1024
MASK_BIG = float(2.0 ** 127)
SQRT_HALF = float(np.sqrt(0.5))
NEG_INF = float("-inf")


def _rms(x, g):
    return x * lax.rsqrt(jnp.mean(x * x, axis=-1, keepdims=True) + EPS) * g


def _gelu(x):
    return 0.5 * x * (1.0 + lax.erf(x * SQRT_HALF))


def _silu(x):
    return x * jax.nn.sigmoid(x)


def _dot(a, b):
    return jnp.dot(a, b, preferred_element_type=F32)


def _dot_nt(a, b):
    return lax.dot_general(a, b, (((1,), (1,)), ((), ())), preferred_element_type=F32)


def _split3(x):
    a = x.astype(BF16)
    r = x - a.astype(F32)
    b = r.astype(BF16)
    c = (r - b.astype(F32)).astype(BF16)
    return a, b, c


def _rope(a, c_tab, s1_tab, s2_tab):
    w = a.shape[-1]
    reps = w // LANE
    ct = jnp.concatenate([c_tab] * reps, axis=1) if reps > 1 else c_tab
    s1 = jnp.concatenate([s1_tab] * reps, axis=1) if reps > 1 else s1_tab
    s2 = jnp.concatenate([s2_tab] * reps, axis=1) if reps > 1 else s2_tab
    half = ROT_DIM // 2
    return a * ct + pltpu.roll(a, w - half, axis=1) * s1 + pltpu.roll(a, half, axis=1) * s2


def _const_spec(shape, single=True):
    nd = len(shape)
    kw = {"pipeline_mode": pl.Buffered(1)} if single else {}
    return pl.BlockSpec(tuple(shape), lambda *a, _nd=nd: (0,) * _nd, **kw)


def _params(sem):
    return pltpu.CompilerParams(dimension_semantics=sem, vmem_limit_bytes=VMEM_LIMIT)


def _layer_a_kernel(x_ref, g_ref, win_ref, lng_ref, lnb_ref, ws_ref, bs_ref, wout_ref, *rest, sample, tm, aw):
    if sample:
        xo_ref, v_ref = rest
    else:
        xo_ref, mix_ref = rest
    x = x_ref[...]
    xn = _rms(x, g_ref[...]).astype(BF16)
    v = _gelu(_dot(xn, win_ref[:, aw:2 * aw]))
    vc = v - jnp.mean(v, axis=-1, keepdims=True)
    v = vc * lax.rsqrt(jnp.mean(vc * vc, axis=-1, keepdims=True) + EPS) * lng_ref[...] + lnb_ref[...]
    if sample:
        v_ref[...] = v
        mix = v * ws_ref[...] + bs_ref[...]
    else:
        vb = v.astype(BF16)
        tri = (lax.broadcasted_iota(jnp.int32, (A_CHUNK, A_CHUNK), 0)
               >= lax.broadcasted_iota(jnp.int32, (A_CHUNK, A_CHUNK), 1))
        for g in range(A_GROUPS):
            wm = jnp.where(tri, ws_ref[g], 0.0).astype(BF16)
            bias = bs_ref[:, g:g + 1]
            for c in range(tm // A_CHUNK):
                blk = vb[c * A_CHUNK:(c + 1) * A_CHUNK, g * A_CHUNK:(g + 1) * A_CHUNK]
                mix_ref[c * A_CHUNK:(c + 1) * A_CHUNK, g * A_CHUNK:(g + 1) * A_CHUNK] = _dot(wm, blk) + bias
        mix = mix_ref[...]
    u = _gelu(_dot(xn, win_ref[:, 0:aw]))
    z = _dot(xn, win_ref[:, 2 * aw:3 * aw])
    y = (u * mix * _silu(z)).astype(BF16)
    xo_ref[...] = _dot(y, wout_ref[...]) + x


def _layer_a(x2d, norm_g, w_in, ln_g, ln_b, w_s, b_s, w_out, *, sample):
    r, d = x2d.shape
    aw = w_out.shape[0]
    tm = r if sample else TM_ROWS
    assert r % tm == 0 and tm % A_CHUNK == 0 or sample
    if sample:
        ws_arg = jnp.repeat(w_s[:, 0, 0], A_CHUNK)[None, :]
        bs_arg = jnp.repeat(b_s[:, 0], A_CHUNK)[None, :]
    else:
        ws_arg = w_s
        bs_arg = b_s.T
    row = pl.BlockSpec((tm, d), lambda i: (i, 0))
    in_specs = [row, _const_spec((1, d)), _const_spec(w_in.shape), _const_spec((1, aw)), _const_spec((1, aw)),
                _const_spec(ws_arg.shape), _const_spec(bs_arg.shape), _const_spec(w_out.shape)]
    if sample:
        out_shape = (jax.ShapeDtypeStruct((r, d), F32), jax.ShapeDtypeStruct((r, aw), F32))
        out_specs = (row, pl.BlockSpec((tm, aw), lambda i: (i, 0)))
        scratch = []
    else:
        out_shape = jax.ShapeDtypeStruct((r, d), F32)
        out_specs = row
        scratch = [pltpu.VMEM((tm, aw), F32)]
    return pl.pallas_call(
        functools.partial(_layer_a_kernel, sample=sample, tm=tm, aw=aw),
        grid=(r // tm,), in_specs=in_specs, out_specs=out_specs, out_shape=out_shape,
        scratch_shapes=scratch, compiler_params=_params(("parallel",)),
        name="layer_a_sample" if sample else "layer_a_prompt",
    )(x2d, norm_g[None, :], w_in, ln_g[None, :], ln_b[None, :], ws_arg, bs_arg, w_out)


def _proj_kernel(x_ref, g_ref, wq_ref, wkv_ref, *rest, prompt, tm, seq):
    if prompt:
        (wkp_ref, cq_ref, s1q_ref, s2q_ref, ck_ref, s1k_ref, s2k_ref,
         qc_ref, qr_ref, kvc_ref, kvs_ref, kvw_ref, ksa_ref, kwp_ref, vst_ref, vwt_ref) = rest
    else:
        (cq_ref, s1q_ref, s2q_ref, ck_ref, s1k_ref, s2k_ref,
         qc_ref, qr_ref, kvc_ref, kvs_ref, kvw_ref) = rest
    scale = HEAD_DIM ** -0.5
    xn = _rms(x_ref[...], g_ref[...]).astype(BF16)
    cq, s1q, s2q = cq_ref[...], s1q_ref[...], s2q_ref[...]
    ck, s1k, s2k = ck_ref[...], s1k_ref[...], s2k_ref[...]
    kv = _dot(xn, wkv_ref[...])
    kvc = kv[:, 0:KVROW]
    kvs = jnp.concatenate([_rope(kv[:, KVROW:KVROW + KVW], ck, s1k, s2k), kv[:, KVROW + KVW:2 * KVROW]], axis=1)
    kvw = jnp.concatenate([_rope(kv[:, 2 * KVROW:2 * KVROW + KVW], ck, s1k, s2k),
                           kv[:, 2 * KVROW + KVW:3 * KVROW]], axis=1)
    q = _dot(xn, wq_ref[...])
    if not prompt:
        qc_ref[...] = (q * scale).astype(BF16)
        qr_ref[...] = (_rope(q, cq, s1q, s2q) * scale).astype(BF16)
        kvc_ref[...] = kvc
        kvs_ref[...] = kvs
        kvw_ref[...] = kvw
        return
    qc_ref[...] = jnp.transpose(q * scale).astype(BF16)
    qr_ref[...] = jnp.transpose(_rope(q, ck, s1k, s2k) * scale).astype(BF16)
    kvc_ref[...] = jnp.transpose(kvc)
    kvs_t = jnp.transpose(kvs)
    kvw_t = jnp.transpose(kvw)
    kvs_ref[...] = kvs_t
    kvw_ref[...] = kvw_t
    ones_pad = jnp.where(lax.broadcasted_iota(jnp.int32, (SLOT - HEAD_DIM, tm), 0) == 0, 1.0, 0.0)
    for g in range(N_KV):
        r0 = KVW + g * HEAD_DIM
        vst_ref[g * SLOT:(g + 1) * SLOT, :] = jnp.concatenate([kvs_t[r0:r0 + HEAD_DIM], ones_pad], axis=0).astype(BF16)
        vwt_ref[g * SLOT:(g + 1) * SLOT, :] = jnp.concatenate([kvw_t[r0:r0 + HEAD_DIM], ones_pad], axis=0).astype(BF16)
    pw = N_KV * SLOT
    kp = _dot(xn, wkp_ref[...])
    lane = lax.broadcasted_iota(jnp.int32, (tm, pw), 1) % SLOT
    pos = (pl.program_id(0) % (seq // tm)) * tm + lax.broadcasted_iota(jnp.int32, (tm, pw), 0)
    onehot = jnp.where(pos // SEL_BLOCK == lane - HEAD_DIM, 1.0, 0.0)
    ksa_ref[...] = jnp.where(lane < HEAD_DIM, _rope(kp[:, 0:pw], cq, s1q, s2q), onehot).astype(BF16)
    kwp_ref[...] = _rope(kp[:, pw:2 * pw], cq, s1q, s2q).astype(BF16)


def _project(x2d, norm_g, wq, wkv, wk_pad, tabs_q, tabs_k, *, prompt, seq):
    r, d = x2d.shape
    tm = TQ if prompt else r
    nt = seq // tm if prompt else 1
    row = lambda w: pl.BlockSpec((tm, w), lambda i: (i, 0))
    tab = pl.BlockSpec((tm, LANE), (lambda i: (i % nt, 0)) if prompt else (lambda i: (0, 0)))
    pw = N_KV * SLOT
    in_specs = [row(d), _const_spec((1, d)), _const_spec(wq.shape), _const_spec(wkv.shape)]
    args = [x2d, norm_g[None, :], wq, wkv]
    if prompt:
        in_specs.append(_const_spec(wk_pad.shape))
        args.append(wk_pad)
    in_specs += [tab] * 6
    args += list(tabs_q) + list(tabs_k)
    if prompt:
        nbatch = r // seq
        qd = N_HEADS * HEAD_DIM
        tile_t = lambda rows: pl.BlockSpec((None, None, rows, tm), lambda i: (i // nt, i % nt, 0, 0))
        out_shape = [jax.ShapeDtypeStruct((nbatch, nt, qd, tm), BF16)] * 2
        out_specs = [tile_t(qd)] * 2
        out_shape += [jax.ShapeDtypeStruct((nbatch, KVROW, seq), F32)] * 3
        out_specs += [pl.BlockSpec((None, KVROW, tm), lambda i: (i // nt, 0, i % nt))] * 3
        out_shape += [jax.ShapeDtypeStruct((r, pw), BF16)] * 2
        out_specs += [row(pw)] * 2
        out_shape += [jax.ShapeDtypeStruct((nbatch, nt, pw, tm), BF16)] * 2
        out_specs += [tile_t(pw)] * 2
    else:
        qw = N_HEADS * SLOT
        out_shape = [jax.ShapeDtypeStruct((r, qw), BF16)] * 2
        out_specs = [row(qw), row(qw)]
        out_shape += [jax.ShapeDtypeStruct((r, KVROW), F32)] * 3
        out_specs += [row(KVROW)] * 3
    return pl.pallas_call(
        functools.partial(_proj_kernel, prompt=prompt, tm=tm, seq=seq),
        grid=(r // tm,), in_specs=in_specs, out_specs=tuple(out_specs), out_shape=tuple(out_shape),
        compiler_params=_params(("parallel",)),
        name="nsa_project_prompt" if prompt else "nsa_project_sample",
    )(*args)


def _cmp_lh_kernel(*refs, n_in, n_scalar):
    refs = refs[n_scalar:]
    x_refs = refs[:n_in]
    w_ref, lh_ref, xs_ref = refs[n_in], refs[n_in + 1], refs[n_in + 2]
    half = w_ref.shape[-1]
    tpb = x_refs[0].shape[-1]
    n_rows = lh_ref.shape[0]
    rows_u = n_rows // 2
    tiles_u = rows_u * CMP_STRIDE // LANE
    units = [(kv, hf) for kv in range(2) for hf in range(2)]

    pitch = xs_ref.shape[2] // CMP_STRIDE
    sub = 8

    def fill(u):
        kv, hf = units[u]
        for tt in range(tiles_u):
            tok0 = (hf * tiles_u + tt) * LANE
            r, t0 = x_refs[tok0 // tpb], tok0 % tpb
            for gp in range(N_KV // 2):
                tile = jnp.concatenate([r[kv, 2 * gp, :, t0:t0 + LANE], r[kv, 2 * gp + 1, :, t0:t0 + LANE]], axis=0)
                tile_t = jnp.transpose(tile)
                for v in range(LANE // sub):
                    j0 = (v * sub) % CMP_STRIDE
                    m = (tt * LANE + v * sub) // CMP_STRIDE
                    xs_ref[u, gp, pl.ds(j0 * pitch + m, sub, stride=pitch), :] = tile_t[v * sub:(v + 1) * sub, :]

    def matmuls(u):
        kv, hf = units[u]
        acc = None
        for j in range(CMP_STRIDE):
            xj = jnp.concatenate([xs_ref[u, gp, j * pitch:j * pitch + rows_u, :]
                                  for gp in range(N_KV // 2)], axis=1).astype(BF16)
            t = _dot(xj, w_ref[2 * j + kv])
            acc = t if acc is None else acc + t
        lh_ref[hf * rows_u:(hf + 1) * rows_u, kv * half:(kv + 1) * half] = acc

    fill(0)
    for u in range(len(units)):
        if u + 1 < len(units):
            fill(u + 1)
        matmuls(u)


def _cmp_lh_prompt(kvc_t, wc):
    nbatch, _, _, _, seq = kvc_t.shape
    tt = min(seq, 2048)
    assert seq % tt == 0 and tt % LANE == 0
    nt = seq // tt
    out_w = 2 * wc.shape[-1]
    return pl.pallas_call(
        functools.partial(_cmp_lh_kernel, n_in=1, n_scalar=0),
        grid=(nbatch, nt),
        in_specs=[pl.BlockSpec((None, 2, N_KV, HEAD_DIM, tt), lambda b, i: (b, 0, 0, 0, i)), _const_spec(wc.shape)],
        out_specs=pl.BlockSpec((tt // CMP_STRIDE, out_w), lambda b, i: (b * nt + i, 0)),
        out_shape=jax.ShapeDtypeStruct((nbatch * seq // CMP_STRIDE, out_w), F32),
        scratch_shapes=[pltpu.VMEM((4, N_KV // 2, CMP_STRIDE * (tt // (2 * CMP_STRIDE) + 4), LANE), F32)],
        compiler_params=_params(("parallel", "parallel")), name="cmp_lh_prompt",
    )(kvc_t, wc)


def _cmp_lh_sample(pool_t, page_flat, wc):
    n_pages = page_flat.shape[0]
    pps = PAGES_PER_STEP
    assert n_pages % pps == 0
    steps = n_pages // pps
    blk = (None,) + tuple(pool_t.shape[1:])
    page_specs = [pl.BlockSpec(blk, (lambda i, pt, _p=p: (pt[i * pps + _p], 0, 0, 0, 0))) for p in range(pps)]
    nd = wc.ndim
    w_spec = pl.BlockSpec(wc.shape, lambda i, pt: (0,) * nd, pipeline_mode=pl.Buffered(1))
    out_w = 2 * wc.shape[-1]
    rows = pps * PAGE_SIZE // CMP_STRIDE
    grid_spec = pltpu.PrefetchScalarGridSpec(
        num_scalar_prefetch=1, grid=(steps,),
        in_specs=page_specs + [w_spec],
        out_specs=pl.BlockSpec((rows, out_w), lambda i, pt: (i, 0)),
        scratch_shapes=[pltpu.VMEM((4, N_KV // 2, CMP_STRIDE * (rows // 2 + 4), LANE), F32)])
    return pl.pallas_call(
        functools.partial(_cmp_lh_kernel, n_in=pps, n_scalar=1),
        grid_spec=grid_spec,
        out_shape=jax.ShapeDtypeStruct((steps * rows, out_w), F32),
        compiler_params=_params(("parallel",)), name="cmp_lh_sample",
    )(page_flat, *([pool_t] * pps), wc)


def _compress_finish(lh, pe_ref, w1f_ref, b1_ref, w2_ref, b2_ref, f32_v=False):
    nc = lh.shape[0]
    outs = []
    for kv in range(2):
        pe8 = jnp.broadcast_to(pe_ref[kv], (8, pe_ref.shape[-1])).astype(BF16)
        cb = _dot(pe8, w1f_ref[kv])[0:1, :] + b1_ref[kv]
        lo = lh[:, kv * 2 * KVW:kv * 2 * KVW + KVW]
        hi = lh[:, kv * 2 * KVW + KVW:(kv + 1) * 2 * KVW]
        hi_next = pltpu.roll(hi, nc - 1, axis=0)
        hid = _silu(lo + hi_next + cb)
        out = _dot(hid.astype(BF16), w2_ref[kv]) + b2_ref[kv]
        outs.append(out if (f32_v and kv == 1) else out.astype(BF16))
    return outs


def _select_blocks(imp, blk, jq, axis):
    forced = (blk == 0) | (blk == jq) | (blk == jq - 1)
    score = jnp.where((blk > jq) | (blk < 0), NEG_INF, jnp.where(forced, jnp.inf, imp))
    blk_f = blk.astype(F32)
    sel = jnp.zeros(imp.shape, F32)
    idxs = []
    for _ in range(N_SELECT):
        m = jnp.max(score, axis=axis, keepdims=True)
        idx = jnp.min(jnp.where(score == m, blk_f, 1e9), axis=axis, keepdims=True)
        hit = (blk_f == idx) & (m > NEG_INF)
        sel = jnp.where(hit, 1.0, sel)
        score = jnp.where(blk_f == idx, NEG_INF, score)
        idxs.append(idx)
    return sel, idxs


def _gate(xn, wz_ref, wgl_ref, gb_ref, e_ref):
    z = _dot(xn, wz_ref[...])
    gl = _dot(xn, wgl_ref[...]) + gb_ref[...]
    e = e_ref[...]
    a, b, c = _split3(gl)
    gle = _dot(a, e) + _dot(b, e) + _dot(c, e)
    return _silu(z), jax.nn.sigmoid(gle)


def _gate_kernel(x_ref, g_ref, wz_ref, wgl_ref, gb_ref, e_ref, sz_ref, gg_ref):
    xn = _rms(x_ref[...], g_ref[...]).astype(BF16)
    sz, gg = _gate(xn, wz_ref, wgl_ref, gb_ref, e_ref)
    sz_ref[...] = sz
    gg_ref[...] = gg


def _out_kernel(y_ref, x_ref, w_ref, g_ref, o_ref):
    o_ref[...] = _rms(_dot(y_ref[...].astype(BF16), w_ref[...]) + x_ref[...], g_ref[...])


def _attn_prompt_kernel(x_ref, qc_ref, qr_ref, lh_ref, ksa_ref, vsp_ref,
                        kw0_ref, kw1_ref, kw2_ref, vw0_ref, vw1_ref, vw2_ref,
                        g1_ref, gf_ref, wz_ref, wgl_ref, gb_ref, e_ref, wout_ref,
                        pe_ref, w1f_ref, b1_ref, w2_ref, b2_ref, ovt_ref,
                        out_ref, kc_scr, vc_scr, o_scr, *, tq, seq):
    qi = pl.program_id(1)
    t0 = qi * tq
    nc = seq // CMP_STRIDE
    rows = Q_PER_KV * tq

    @pl.when(qi == 0)
    def _():
        kc, vc = _compress_finish(lh_ref[...], pe_ref, w1f_ref, b1_ref, w2_ref, b2_ref)
        kc_scr[...] = kc
        vc_scr[...] = vc

    a_row = lax.broadcasted_iota(jnp.int32, (rows, 1), 0) % tq
    pos_row = t0 + a_row
    c_col = lax.broadcasted_iota(jnp.int32, (1, tq), 1)
    n_col = lax.broadcasted_iota(jnp.int32, (1, nc), 1)
    mask_c = (n_col * CMP_STRIDE + CMP_BLOCK - 1) <= pos_row
    causal = c_col <= a_row
    blk_t = lax.broadcasted_iota(jnp.int32, (2 * SEL_BLOCK, tq), 0) - SEL_BLOCK
    jq_t = (t0 + lax.broadcasted_iota(jnp.int32, (2 * SEL_BLOCK, tq), 1)) // SEL_BLOCK
    lane_q = lax.broadcasted_iota(jnp.int32, (tq, SLOT), 1)
    tiny = jnp.finfo(jnp.float32).tiny

    def stack(ref, g):
        return jnp.concatenate([ref[:, (Q_PER_KV * g + r) * SLOT:(Q_PER_KV * g + r + 1) * SLOT]
                                for r in range(Q_PER_KV)], axis=0)

    def put(br, g, o):
        for j in range(Q_PER_KV // 2):
            pair = jnp.concatenate([o[(2 * j) * tq:(2 * j + 1) * tq, 0:HEAD_DIM],
                                    o[(2 * j + 1) * tq:(2 * j + 2) * tq, 0:HEAD_DIM]], axis=1)
            h0 = Q_PER_KV * g + 2 * j
            o_scr[br, :, h0 * HEAD_DIM:(h0 + 2) * HEAD_DIM] = pair

    for g in range(N_KV):
        gs = slice(g * SLOT, (g + 1) * SLOT)
        s = _dot_nt(stack(qc_ref, g), kc_scr[:, gs])
        s = jnp.where(mask_c, s, NEG_INF)
        m = jnp.max(s, axis=-1, keepdims=True)
        m = jnp.where(m > NEG_INF, m, 0.0)
        e = jnp.where(mask_c, jnp.exp(s - m), 0.0)
        p = e / jnp.maximum(jnp.sum(e, axis=-1, keepdims=True), tiny)
        put(0, g, _dot(p.astype(BF16), vc_scr[:, gs]))
        psum = p[0:tq] + p[tq:2 * tq] + p[2 * tq:3 * tq] + p[3 * tq:4 * tq]
        ovt = ovt_ref[...]
        pa, pb, pc = _split3(psum)
        imp_t = _dot_nt(ovt, pa) + _dot_nt(ovt, pb) + _dot_nt(ovt, pc)
        sel_t, _ = _select_blocks(imp_t, blk_t, jq_t, axis=0)
        sel = jnp.transpose(sel_t)
        maskpart = jnp.where(lane_q >= HEAD_DIM, (sel - 1.0) * MASK_BIG, 0.0).astype(BF16)
        qr_g = stack(qr_ref, g)
        qa = jnp.where(jnp.concatenate([lane_q] * Q_PER_KV, axis=0) < HEAD_DIM, qr_g,
                       jnp.concatenate([maskpart] * Q_PER_KV, axis=0))

        def sel_tile(kt, carry, diag):
            m_i, l_i, acc = carry
            start = pl.multiple_of(kt * tq, tq)
            k_t = ksa_ref[kt, g * SLOT:(g + 1) * SLOT, :]
            v = vsp_ref[pl.ds(start, tq), gs]
            st = _dot(qa, k_t)
            if diag:
                st = jnp.where(causal, st, -MASK_BIG)
            m_new = jnp.maximum(m_i, jnp.max(st, axis=-1, keepdims=True))
            alpha = jnp.exp(m_i - m_new)
            pt = jnp.exp(st - m_new)
            l_new = alpha * l_i + jnp.sum(pt, axis=-1, keepdims=True)
            acc_new = alpha * acc + _dot(pt.astype(BF16), v)
            return m_new, l_new, acc_new

        init = (jnp.full((rows, 1), NEG_INF, F32), jnp.zeros((rows, 1), F32), jnp.zeros((rows, SLOT), F32))
        carry = lax.fori_loop(0, qi, lambda kt, c: sel_tile(kt, c, False), init)
        _, l_s, acc_s = sel_tile(qi, carry, True)
        put(1, g, acc_s / l_s)
        s0 = _dot(qr_g, kw0_ref[gs, :])
        s1 = _dot(qr_g, kw1_ref[gs, :])
        s2 = _dot(qr_g, kw2_ref[gs, :])
        s0 = jnp.where((c_col > a_row) & (qi >= 2), s0, NEG_INF)
        s1 = jnp.where(qi >= 1, s1, NEG_INF)
        s2 = jnp.where(causal, s2, NEG_INF)
        mw = jnp.maximum(jnp.maximum(jnp.max(s0, axis=-1, keepdims=True), jnp.max(s1, axis=-1, keepdims=True)),
                         jnp.max(s2, axis=-1, keepdims=True))
        p0, p1, p2 = jnp.exp(s0 - mw), jnp.exp(s1 - mw), jnp.exp(s2 - mw)
        lw = (jnp.sum(p0, axis=-1, keepdims=True) + jnp.sum(p1, axis=-1, keepdims=True)
              + jnp.sum(p2, axis=-1, keepdims=True))
        ow = (_dot(p0.astype(BF16), vw0_ref[:, gs]) + _dot(p1.astype(BF16), vw1_ref[:, gs])
              + _dot(p2.astype(BF16), vw2_ref[:, gs]))
        put(2, g, ow / lw)

    x = x_ref[...]
    xn = _rms(x, g1_ref[...]).astype(BF16)
    sz, gg = _gate(xn, wz_ref, wgl_ref, gb_ref, e_ref)
    qw = N_HEADS * HEAD_DIM
    y = None
    for br in range(N_BRANCH):
        t = o_scr[br] * sz[:, br * qw:(br + 1) * qw] * gg[:, br * qw:(br + 1) * qw]
        y = t if y is None else y + t
    out_ref[...] = _rms(_dot(y.astype(BF16), wout_ref[...]) + x, gf_ref[...])


def _attn_prompt(x1, qc, qr, lh, ksa, vsp, kwp, vwp, g1, gf, wz, wgl, gb, e_mat, wout,
                 pe, w1f, b1, w2p, b2p, ovt, *, batch, seq):
    tq = TQ
    assert seq % tq == 0 and WINDOW == 2 * tq and seq % SEL_BLOCK == 0 and seq // SEL_BLOCK <= SEL_BLOCK
    nq = seq // tq
    nc = seq // CMP_STRIDE
    d = x1.shape[1]
    qw = N_HEADS * SLOT
    pw = N_KV * SLOT
    tile = lambda w: pl.BlockSpec((tq, w), lambda b, i: (b * nq + i, 0))
    per_b = lambda rows, w: pl.BlockSpec((rows, w), lambda b, i: (b, 0), pipeline_mode=pl.Buffered(1))
    ksa_spec = pl.BlockSpec((None, nq, pw, tq), lambda b, i: (b, 0, 0, 0), pipeline_mode=pl.Buffered(1))
    kwin = lambda off: pl.BlockSpec((None, None, pw, tq), lambda b, i: (b, jnp.maximum(i - off, 0), 0, 0))
    win = lambda off: pl.BlockSpec((tq, pw), lambda b, i: (b * nq + jnp.maximum(i - off, 0), 0))
    consts = [g1[None, :], gf[None, :], wz, wgl, gb, e_mat, wout, pe, w1f, b1, w2p, b2p, ovt]
    in_specs = ([tile(d), tile(qw), tile(qw), per_b(nc, lh.shape[1]), ksa_spec, per_b(seq, pw),
                 kwin(2), kwin(1), kwin(0), win(2), win(1), win(0)]
                + [_const_spec(c.shape) for c in consts])
    return pl.pallas_call(
        functools.partial(_attn_prompt_kernel, tq=tq, seq=seq),
        grid=(batch, nq), in_specs=in_specs, out_specs=tile(d),
        out_shape=jax.ShapeDtypeStruct((batch * seq, d), F32),
        scratch_shapes=[pltpu.VMEM((nc, pw), BF16), pltpu.VMEM((nc, pw), BF16),
                        pltpu.VMEM((N_BRANCH, tq, N_HEADS * HEAD_DIM), F32)],
        compiler_params=_params(("arbitrary", "arbitrary")), name="nsa_attend_prompt",
    )(x1, qc, qr, lh, ksa, vsp, kwp, kwp, kwp, vwp, vwp, vwp, *consts)


def _attn_t_kernel(x_ref, qc_ref, qr_ref, lh_ref, ksa_ref, vst_ref,
                   kw0_ref, kw1_ref, kw2_ref, vw0_ref, vw1_ref, vw2_ref,
                   g1_ref, gf_ref, wz_ref, wgl_ref, gb_ref, wout_ref,
                   pe_ref, w1f_ref, b1_ref, w2_ref, b2_ref, ov_ref,
                   out_ref, kc_scr, vct_scr, o_scr, *, tq, seq):
    qi = pl.program_id(1)
    t0 = qi * tq
    nc = seq // CMP_STRIDE
    cols = Q_PER_KV * tq
    hd = HEAD_DIM

    @pl.when(qi == 0)
    def _():
        kc, vc = _compress_finish(lh_ref[...], pe_ref, w1f_ref, b1_ref, w2_ref, b2_ref, f32_v=True)
        kc_scr[...] = kc
        vct_scr[...] = jnp.transpose(vc).astype(BF16)

    a_col = lax.broadcasted_iota(jnp.int32, (1, cols), 1) % tq
    pos_col = t0 + a_col
    c_row = lax.broadcasted_iota(jnp.int32, (tq, 1), 0)
    n_row = lax.broadcasted_iota(jnp.int32, (nc, 1), 0)
    mask_c = (n_row * CMP_STRIDE + CMP_BLOCK - 1) <= pos_col
    causal = c_row <= a_col
    blk_t = lax.broadcasted_iota(jnp.int32, (SEL_BLOCK, tq), 0)
    jq_t = (t0 + lax.broadcasted_iota(jnp.int32, (SEL_BLOCK, tq), 1)) // SEL_BLOCK
    tiny = jnp.finfo(jnp.float32).tiny
    zpad = jnp.zeros((SLOT - hd, tq), BF16)

    def stack_t(ref, g, lower):
        return jnp.concatenate(
            [jnp.concatenate([ref[(Q_PER_KV * g + r) * hd:(Q_PER_KV * g + r + 1) * hd, :], lower], axis=0)
             for r in range(Q_PER_KV)], axis=1)

    def put(br, g, o_t):
        for r in range(Q_PER_KV):
            h = Q_PER_KV * g + r
            o_scr[br, h * hd:(h + 1) * hd, :] = o_t[:, r * tq:(r + 1) * tq]

    gsl = lambda g: slice(g * SLOT, (g + 1) * SLOT)

    def cmp_scores(g):
        return _dot(kc_scr[:, gsl(g)], stack_t(qc_ref, g, zpad))

    def cmp_finish(g, s):
        s = jnp.where(mask_c, s, NEG_INF)
        m = jnp.max(s, axis=0, keepdims=True)
        m = jnp.where(m > NEG_INF, m, 0.0)
        e = jnp.exp(s - m)
        p = e * (1.0 / jnp.maximum(jnp.sum(e, axis=0, keepdims=True), tiny))
        put(0, g, _dot(vct_scr[gsl(g), :], p.astype(BF16))[0:hd])
        psum = p[:, 0:tq] + p[:, tq:2 * tq] + p[:, 2 * tq:3 * tq] + p[:, 3 * tq:4 * tq]
        ov = ov_ref[...]
        pa, pb, pc = _split3(psum)
        return _dot(ov, pa) + _dot(ov, pb) + _dot(ov, pc)

    def win_scores(g):
        qr_t = stack_t(qr_ref, g, zpad)
        return (_dot(kw0_ref[:, gsl(g)], qr_t), _dot(kw1_ref[:, gsl(g)], qr_t), _dot(kw2_ref[:, gsl(g)], qr_t))

    def win_finish(g, ss):
        s0 = jnp.where((c_row > a_col) & (qi >= 2), ss[0], NEG_INF)
        s1 = jnp.where(qi >= 1, ss[1], NEG_INF)
        s2 = jnp.where(causal, ss[2], NEG_INF)
        mw = jnp.maximum(jnp.maximum(jnp.max(s0, axis=0, keepdims=True), jnp.max(s1, axis=0, keepdims=True)),
                         jnp.max(s2, axis=0, keepdims=True))
        acc_w = (_dot(vw0_ref[gsl(g), :], jnp.exp(s0 - mw).astype(BF16))
                 + _dot(vw1_ref[gsl(g), :], jnp.exp(s1 - mw).astype(BF16))
                 + _dot(vw2_ref[gsl(g), :], jnp.exp(s2 - mw).astype(BF16)))
        put(2, g, acc_w[0:hd] * (1.0 / acc_w[hd:hd + 1]))

    imps = [None] * N_KV
    s_c = cmp_scores(0)
    for g in range(N_KV):
        s_next = cmp_scores(g + 1) if g + 1 < N_KV else win_scores(0)
        imps[g] = cmp_finish(g, s_c)
        s_c = s_next
    sel_all, _ = _select_blocks(jnp.concatenate(imps, axis=1), jnp.concatenate([blk_t] * N_KV, axis=1),
                                jnp.concatenate([jq_t] * N_KV, axis=1), axis=0)
    maskbias_all = ((sel_all - 1.0) * MASK_BIG).astype(BF16)
    s_w = s_c
    for g in range(N_KV):
        s_next = win_scores(g + 1) if g + 1 < N_KV else None
        win_finish(g, s_w)
        s_w = s_next

    for g in range(N_KV):
        gs = slice(g * SLOT, (g + 1) * SLOT)
        qa_t = stack_t(qr_ref, g, maskbias_all[:, g * tq:(g + 1) * tq])

        def scores(kt):
            start = pl.multiple_of(kt * tq, tq)
            return _dot(ksa_ref[pl.ds(start, tq), gs], qa_t)

        def consume(kt, st, m_i, acc):
            m_new = jnp.maximum(m_i, jnp.max(st, axis=0, keepdims=True))
            alpha = jnp.exp(m_i - m_new)
            pt = jnp.exp(st - m_new).astype(BF16)
            return m_new, alpha * acc + _dot(vst_ref[kt, gs, :], pt)

        def pair(j, carry):
            m0, a0, m1, a1 = carry
            st0, st1 = scores(2 * j), scores(2 * j + 1)
            m0, a0 = consume(2 * j, st0, m0, a0)
            m1, a1 = consume(2 * j + 1, st1, m1, a1)
            return m0, a0, m1, a1

        m_init = jnp.full((1, cols), NEG_INF, F32)
        a_init = jnp.zeros((SLOT, cols), F32)
        m0, a0, m1, a1 = lax.fori_loop(0, qi // 2, pair, (m_init, a_init, m_init, a_init))
        odd = (qi % 2) == 1
        e0 = qi - qi % 2
        m0, a0 = consume(e0, jnp.where(causal | odd, scores(e0), -MASK_BIG), m0, a0)
        m1, a1 = lax.cond(odd, lambda: consume(qi, jnp.where(causal, scores(qi), -MASK_BIG), m1, a1),
                          lambda: (m1, a1))
        m_s = jnp.maximum(m0, m1)
        acc_s = a0 * jnp.exp(m0 - m_s) + a1 * jnp.exp(m1 - m_s)
        put(1, g, acc_s[0:hd] * (1.0 / acc_s[hd:hd + 1]))

    x = x_ref[...]
    xn_t = jnp.transpose(_rms(x, g1_ref[...])).astype(BF16)
    gate = jax.nn.sigmoid(_dot(wgl_ref[...], xn_t) + jnp.concatenate([gb_ref[...]] * (tq // LANE), axis=1))
    qw = N_HEADS * hd
    y_t = None
    for br in range(N_BRANCH):
        sz = _silu(_dot(wz_ref[br * qw:(br + 1) * qw, :], xn_t))
        parts = []
        for h in range(N_HEADS):
            j = br * N_HEADS + h
            parts.append(o_scr[br, h * hd:(h + 1) * hd, :] * sz[h * hd:(h + 1) * hd, :] * gate[j:j + 1, :])
        t = jnp.concatenate(parts, axis=0)
        y_t = t if y_t is None else y_t + t
    o = jnp.transpose(_dot(wout_ref[...], y_t.astype(BF16)))
    out_ref[...] = _rms(o + x, gf_ref[...])


def _attn_prompt_t(x1, qc_t, qr_t, lh, ksa, vst, kwp, vwt, g1, gf, wz_t, wgl_t, gb_b, wout_t,
                   pe, w1f, b1, w2p, b2p, ov, *, batch, seq):
    tq = TQ
    assert seq % tq == 0 and WINDOW == 2 * tq and seq % SEL_BLOCK == 0 and seq // SEL_BLOCK <= SEL_BLOCK
    nq = seq // tq
    nc = seq // CMP_STRIDE
    d = x1.shape[1]
    pw = N_KV * SLOT
    qd = N_HEADS * HEAD_DIM
    tile = lambda w: pl.BlockSpec((tq, w), lambda b, i: (b * nq + i, 0))
    tile_t = lambda rows: pl.BlockSpec((None, None, rows, tq), lambda b, i: (b, i, 0, 0))
    per_b = lambda rows, w: pl.BlockSpec((rows, w), lambda b, i: (b, 0), pipeline_mode=pl.Buffered(1))
    vst_spec = pl.BlockSpec((None, nq, pw, tq), lambda b, i: (b, 0, 0, 0), pipeline_mode=pl.Buffered(1))
    kwin = lambda off: pl.BlockSpec((tq, pw), lambda b, i: (b * nq + jnp.maximum(i - off, 0), 0))
    vwin = lambda off: pl.BlockSpec((None, None, pw, tq), lambda b, i: (b, jnp.maximum(i - off, 0), 0, 0))
    consts = [g1[None, :], gf[None, :], wz_t, wgl_t, gb_b, wout_t, pe, w1f, b1, w2p, b2p, ov]
    in_specs = ([tile(d), tile_t(qd), tile_t(qd), per_b(nc, lh.shape[1]), per_b(seq, pw), vst_spec,
                 kwin(2), kwin(1), kwin(0), vwin(2), vwin(1), vwin(0)]
                + [_const_spec(c.shape) for c in consts])
    return pl.pallas_call(
        functools.partial(_attn_t_kernel, tq=tq, seq=seq),
        grid=(batch, nq), in_specs=in_specs, out_specs=tile(d),
        out_shape=jax.ShapeDtypeStruct((batch * seq, d), F32),
        scratch_shapes=[pltpu.VMEM((nc, pw), BF16), pltpu.VMEM((pw, nc), BF16),
                        pltpu.VMEM((N_BRANCH, qd, tq), F32)],
        compiler_params=_params(("arbitrary", "arbitrary")), name="nsa_attend_prompt",
    )(x1, qc_t, qr_t, lh, ksa, vst, kwp, kwp, kwp, vwt, vwt, vwt, *consts)


def _sample_cmp_kernel(lh_ref, qc_ref, pe_ref, w1f_ref, b1_ref, w2_ref, b2_ref, ov_ref,
                       topi_ref, oc_ref, *, past):
    kc, vc = _compress_finish(lh_ref[...], pe_ref, w1f_ref, b1_ref, w2_ref, b2_ref)
    nc = kc.shape[0]
    nbp = ov_ref.shape[1]
    q = qc_ref[...]
    head_row = lax.broadcasted_iota(jnp.int32, (N_HEADS, 1), 0)
    n_col = lax.broadcasted_iota(jnp.int32, (1, nc), 1)
    mask_c = (n_col * CMP_STRIDE + CMP_BLOCK - 1) <= past
    row8 = lax.broadcasted_iota(jnp.int32, (8, 1), 0)
    tiny = jnp.finfo(jnp.float32).tiny
    oc = jnp.zeros((N_HEADS, SLOT), F32)
    psum8 = jnp.zeros((8, nc), F32)
    for g in range(N_KV):
        gs = slice(g * SLOT, (g + 1) * SLOT)
        in_g = (head_row // Q_PER_KV) == g
        s = jnp.where(mask_c, _dot_nt(q, kc[:, gs]), NEG_INF)
        m = jnp.max(s, axis=-1, keepdims=True)
        m = jnp.where(m > NEG_INF, m, 0.0)
        e = jnp.where(mask_c, jnp.exp(s - m), 0.0)
        p = e / jnp.maximum(jnp.sum(e, axis=-1, keepdims=True), tiny)
        oc = jnp.where(in_g, _dot(p.astype(BF16), vc[:, gs]), oc)
        pg = jnp.sum(jnp.where(in_g, p, 0.0), axis=0, keepdims=True)
        psum8 = jnp.where(row8 == g, pg, psum8)
    ov = ov_ref[...]
    pa, pb, pc = _split3(psum8)
    imp = _dot(pa, ov) + _dot(pb, ov) + _dot(pc, ov)
    blk = lax.broadcasted_iota(jnp.int32, (8, nbp), 1)
    n_blk = past // SEL_BLOCK + 1
    blk = jnp.where(blk < n_blk, blk, -1)
    _, idxs = _select_blocks(imp, blk, jnp.int32(past // SEL_BLOCK), axis=1)
    lane = lax.broadcasted_iota(jnp.int32, (8, LANE), 1)
    topi = jnp.zeros((8, LANE), jnp.int32)
    for r, idx in enumerate(idxs):
        topi = jnp.where(lane == r, idx.astype(jnp.int32), topi)
    topi_ref[...] = topi
    oc_ref[...] = oc


def _sample_cmp(lh3, qc3, pe, w1f, b1, w2p, b2p, ov, *, past):
    nb, nc, w = lh3.shape
    consts = [pe, w1f, b1, w2p, b2p, ov]
    return pl.pallas_call(
        functools.partial(_sample_cmp_kernel, past=past),
        grid=(nb,),
        in_specs=[pl.BlockSpec((None, nc, w), lambda b: (b, 0, 0)),
                  pl.BlockSpec((None, N_HEADS, SLOT), lambda b: (b, 0, 0))]
                 + [_const_spec(c.shape) for c in consts],
        out_specs=(pl.BlockSpec((None, 8, LANE), lambda b: (b, 0, 0)),
                   pl.BlockSpec((None, N_HEADS, SLOT), lambda b: (b, 0, 0))),
        out_shape=(jax.ShapeDtypeStruct((nb, 8, LANE), jnp.int32),
                   jax.ShapeDtypeStruct((nb, N_HEADS, SLOT), F32)),
        compiler_params=_params(("parallel",)), name="nsa_sample_cmp_topk",
    )(lh3, qc3, *consts)


def _sample_attn_kernel(pt_ref, ti_ref, *refs, n_pool_blk, wlen):
    k_refs = refs[:N_SELECT]
    v_refs = refs[N_SELECT:2 * N_SELECT]
    (qr_ref, ksn_ref, vsn_ref, kwn_ref, vwn_ref, win_ref, wnew_ref, oc_ref, sz_ref, gg_ref,
     y_ref, wout_ref) = refs[2 * N_SELECT:]
    b = pl.program_id(0)
    g = pl.program_id(1)
    per_page = PAGE_SIZE // SEL_BLOCK
    q = qr_ref[...]
    qf = q[:, 0:HEAD_DIM].astype(F32)
    in_g = (lax.broadcasted_iota(jnp.int32, (N_HEADS, 1), 0) // Q_PER_KV) == g
    zpad = jnp.zeros((SLOT - HEAD_DIM, PAGE_SIZE), BF16)
    tok_half = lax.broadcasted_iota(jnp.int32, (1, PAGE_SIZE), 1) // SEL_BLOCK

    s_new = jnp.sum(qf * ksn_ref[...], axis=-1, keepdims=True)
    scores = []
    m = s_new
    for r in range(N_SELECT):
        t = ti_ref[(b * N_KV + g) * N_SELECT + r]
        ok = (tok_half == t % per_page) & (t < n_pool_blk)
        k_t = jnp.concatenate([k_refs[r][...].astype(BF16), zpad], axis=0)
        s = jnp.where(ok, _dot(q, k_t), NEG_INF)
        scores.append(s)
        m = jnp.maximum(m, jnp.max(s, axis=-1, keepdims=True))
    p_new = jnp.exp(s_new - m)
    l = p_new
    o = p_new * vsn_ref[...]
    for r in range(N_SELECT):
        p = jnp.exp(scores[r] - m)
        l = l + jnp.sum(p, axis=-1, keepdims=True)
        o = o + _dot_nt(p.astype(BF16), v_refs[r][...].astype(BF16))
    o_s = o / l

    wkey = lax.broadcasted_iota(jnp.int32, (1, wlen), 1)
    kw_t = jnp.concatenate([win_ref[0].astype(BF16), jnp.zeros((SLOT - HEAD_DIM, wlen), BF16)], axis=0)
    s = jnp.where(wkey > wlen - WINDOW, _dot(q, kw_t), NEG_INF)
    s_new = jnp.sum(qf * kwn_ref[...], axis=-1, keepdims=True)
    m = jnp.maximum(jnp.max(s, axis=-1, keepdims=True), s_new)
    p = jnp.exp(s - m)
    p_new = jnp.exp(s_new - m)
    l = jnp.sum(p, axis=-1, keepdims=True) + p_new
    o_w = (_dot_nt(p.astype(BF16), win_ref[1].astype(BF16)) + p_new * vwn_ref[...]) / l

    o_c = oc_ref[:, 0:HEAD_DIM]
    y = o_c * sz_ref[0] * gg_ref[0] + o_s * sz_ref[1] * gg_ref[1] + o_w * sz_ref[2] * gg_ref[2]

    @pl.when(g == 0)
    def _():
        y_ref[...] = jnp.zeros(y_ref.shape, F32)

    y_ref[...] = jnp.where(in_g, y, y_ref[...])
    for kv in range(2):
        shifted = pltpu.roll(win_ref[kv], wlen - 1, axis=1)
        wout_ref[kv] = jnp.where(wkey == wlen - 1, wnew_ref[kv], shifted)


def _sample_attn(page_flat, topi_flat, pool_t, qr3, ksn, vsn, kwn, vwn, win_t, wnew, oc, sz4, gg4, *, n_pages):
    nb = qr3.shape[0]
    wlen = win_t.shape[-1]
    per_page = PAGE_SIZE // SEL_BLOCK
    n_pool_blk = n_pages * per_page

    def pool_map(r, kv):
        def f(b, g, pt, ti):
            blk = jnp.minimum(ti[(b * N_KV + g) * N_SELECT + r], n_pool_blk - 1)
            return (pt[b * n_pages + blk // per_page], kv, g, 0, 0)
        return f

    b3 = lambda s1, s2: pl.BlockSpec((None, s1, s2), lambda b, g, pt, ti: (b, 0, 0))
    b4 = pl.BlockSpec((None, N_BRANCH, N_HEADS, HEAD_DIM), lambda b, g, pt, ti: (b, 0, 0, 0))
    page = lambda r, kv: pl.BlockSpec((None, None, None, HEAD_DIM, PAGE_SIZE), pool_map(r, kv))
    wspec = lambda last: pl.BlockSpec((None, 2, None, HEAD_DIM, last), lambda b, g, pt, ti: (b, 0, g, 0, 0))
    in_specs = ([page(r, 0) for r in range(N_SELECT)] + [page(r, 1) for r in range(N_SELECT)]
                + [b3(N_HEADS, SLOT)] + [b3(N_HEADS, HEAD_DIM)] * 4
                + [wspec(wlen), wspec(1), b3(N_HEADS, SLOT), b4, b4])
    grid_spec = pltpu.PrefetchScalarGridSpec(
        num_scalar_prefetch=2, grid=(nb, N_KV), in_specs=in_specs,
        out_specs=(b3(N_HEADS, HEAD_DIM), wspec(wlen)))
    return pl.pallas_call(
        functools.partial(_sample_attn_kernel, n_pool_blk=n_pool_blk, wlen=wlen),
        grid_spec=grid_spec,
        out_shape=(jax.ShapeDtypeStruct((nb, N_HEADS, HEAD_DIM), F32),
                   jax.ShapeDtypeStruct(win_t.shape, F32)),
        compiler_params=_params(("arbitrary", "arbitrary")), name="nsa_sample_sel_win",
    )(page_flat, topi_flat, *([pool_t] * (2 * N_SELECT)), qr3, ksn, vsn, kwn, vwn, win_t, wnew, oc, sz4, gg4)


def _rope_tables(pos):
    half = ROT_DIM // 2
    freqs = jnp.exp(-math.log(ROPE_THETA) * jnp.arange(half, dtype=F32) * (2.0 / ROT_DIM))
    ang = pos.astype(F32)[:, None] * freqs[None, :]
    cos, sin = jnp.cos(ang), jnp.sin(ang)
    n = pos.shape[0]
    one = jnp.ones((n, HEAD_DIM - ROT_DIM), F32)
    zero8 = jnp.zeros((n, half), F32)
    zrest = jnp.zeros((n, HEAD_DIM - ROT_DIM), F32)
    c64 = jnp.concatenate([cos, cos, one], axis=1)
    s1_64 = jnp.concatenate([-sin, zero8, zrest], axis=1)
    s2_64 = jnp.concatenate([zero8, sin, zrest], axis=1)
    pad1 = jnp.ones((n, SLOT - HEAD_DIM), F32)
    pad0 = jnp.zeros((n, SLOT - HEAD_DIM), F32)
    tabs_q = (jnp.concatenate([c64, pad1], axis=1), jnp.concatenate([s1_64, pad0], axis=1),
              jnp.concatenate([s2_64, pad0], axis=1))
    tabs_k = tuple(jnp.concatenate([t, t], axis=1) for t in (c64, s1_64, s2_64))
    return tabs_q, tabs_k


def _pad_heads(w, n_heads):
    k = w.shape[0]
    w3 = w.reshape(k, n_heads, HEAD_DIM)
    return jnp.pad(w3, ((0, 0), (0, 0), (0, SLOT - HEAD_DIM))).reshape(k, n_heads * SLOT)


def _overlap(nc, nb):
    n = np.arange(nc)[:, None] * CMP_STRIDE
    j = np.arange(nb)[None, :] * SEL_BLOCK
    return ((n <= j + SEL_BLOCK - 1) & (n + CMP_BLOCK - 1 >= j)).astype(np.float32)


def kernel(x_prompt, x_sample, cache_cmp_kv, cache_sel_kv, state_win_kv, page_table, norm_g, final_norm_g,
           a_w_in, a_ln_g, a_ln_b, a_w_s, a_b_s, a_w_out, b_w_in, b_cmp_pe, b_cmp_w1, b_cmp_b1, b_cmp_w2,
           b_cmp_b2, b_gate_b, b_w_out):
    batch, seq, d = x_prompt.shape
    nb, dec_seq, _ = x_sample.shape
    assert dec_seq == 1
    n_pages = page_table.shape[1]
    past = n_pages * PAGE_SIZE
    assert past % SEL_BLOCK == 0 and past % CMP_STRIDE == 0 and past // SEL_BLOCK + 1 >= N_SELECT
    wlen = state_win_kv.shape[2]
    qw = N_HEADS * HEAD_DIM

    a_win = a_w_in[0].astype(BF16)
    a_wout = a_w_out[0].astype(BF16)
    xp1 = _layer_a(x_prompt.reshape(batch * seq, d), norm_g[0], a_win, a_ln_g[0], a_ln_b[0], a_w_s[0], a_b_s[0],
                   a_wout, sample=False)
    xs1, chunk_v = _layer_a(x_sample.reshape(nb, d), norm_g[0], a_win, a_ln_g[0], a_ln_b[0], a_w_s[0], a_b_s[0],
                            a_wout, sample=True)

    w_in = b_w_in[0]
    o1 = qw
    o2 = o1 + N_BRANCH * KVROW
    o3 = o2 + N_BRANCH * qw
    wq_pad = _pad_heads(w_in[:, :o1], N_HEADS).astype(BF16)
    wkv = w_in[:, o1:o2].astype(BF16)
    wq = w_in[:, :o1].astype(BF16)
    wk_sel = w_in[:, o1 + KVROW:o1 + KVROW + KVW]
    wk_win = w_in[:, o1 + 2 * KVROW:o1 + 2 * KVROW + KVW]
    wk_pad = jnp.concatenate([_pad_heads(w, N_KV) for w in (wk_sel, wk_win)], axis=1).astype(BF16)
    wz = w_in[:, o2:o3].astype(BF16)
    n_gate = N_BRANCH * N_HEADS
    n_gate_pad = -(-n_gate // 16) * 16
    wgl = jnp.pad(w_in[:, o3:], ((0, 0), (0, LANE - n_gate))).astype(BF16)
    gb = jnp.pad(b_gate_b[0].reshape(1, n_gate), ((0, 0), (0, LANE - n_gate)))
    wz_t = jnp.transpose(w_in[:, o2:o3]).astype(BF16)
    wgl_t = jnp.pad(jnp.transpose(w_in[:, o3:]), ((0, n_gate_pad - n_gate), (0, 0))).astype(BF16)
    gb_b = jnp.broadcast_to(jnp.pad(b_gate_b[0].reshape(n_gate), (0, n_gate_pad - n_gate))[:, None],
                            (n_gate_pad, LANE))
    wout_t = jnp.transpose(b_w_out[0]).astype(BF16)
    e_np = np.zeros((LANE, n_gate * HEAD_DIM), np.float32)
    for j in range(n_gate):
        e_np[j, j * HEAD_DIM:(j + 1) * HEAD_DIM] = 1.0
    e_mat = jnp.asarray(e_np, dtype=BF16)
    wout = b_w_out[0].astype(BF16)
    w1 = b_cmp_w1[0]
    eye = jnp.eye(N_KV, dtype=F32)
    bd = jnp.einsum("Gg,kjch->kjGcgh", eye, w1).reshape(2, CMP_BLOCK, KVW, KVW)
    wc = jnp.concatenate([bd[:, :CMP_STRIDE], bd[:, CMP_STRIDE:]], axis=-1)
    wc = jnp.transpose(wc, (1, 0, 2, 3)).reshape(2 * CMP_STRIDE, KVW, 2 * KVW).astype(BF16)
    pe = b_cmp_pe[0].reshape(2, 1, CMP_BLOCK * HEAD_DIM)
    w1f = jnp.tile(w1.reshape(2, CMP_BLOCK * HEAD_DIM, HEAD_DIM), (1, 1, N_KV)).astype(BF16)
    b1 = jnp.tile(b_cmp_b1[0], (1, N_KV)).reshape(2, 1, KVW)
    w2bd = jnp.einsum("Gg,kch->kGcgh", eye, b_cmp_w2[0])
    w2p = jnp.pad(w2bd, ((0, 0),) * 4 + ((0, SLOT - HEAD_DIM),)).reshape(2, KVW, N_KV * SLOT).astype(BF16)
    b2p = jnp.pad(jnp.broadcast_to(b_cmp_b2[0][:, None, :], (2, N_KV, HEAD_DIM)),
                  ((0, 0), (0, 0), (0, SLOT - HEAD_DIM))).reshape(2, 1, N_KV * SLOT)

    tabs_q, tabs_k = _rope_tables(jnp.arange(seq, dtype=jnp.int32))
    (qc_t, qr_t, kvc_t, kvs_t, kvw_t, ksa, kwp, vst, vwt) = _project(xp1, norm_g[1], wq, wkv, wk_pad, tabs_q, tabs_k,
                                                                    prompt=True, seq=seq)
    fm = lambda t: t.reshape(t.shape[0], 2, N_KV, HEAD_DIM, t.shape[-1])
    lh_p = _cmp_lh_prompt(fm(kvc_t), wc)
    ov_t = np.zeros((SEL_BLOCK, seq // CMP_STRIDE), np.float32)
    ov_t[:seq // SEL_BLOCK] = _overlap(seq // CMP_STRIDE, seq // SEL_BLOCK).T
    y_prompt = _attn_prompt_t(xp1, qc_t, qr_t, lh_p, ksa, vst, kwp, vwt, norm_g[1], final_norm_g, wz_t, wgl_t, gb_b,
                              wout_t, pe, w1f, b1, w2p, b2p, jnp.asarray(ov_t, dtype=BF16), batch=batch, seq=seq)

    pos_s = jnp.full((nb,), past, dtype=jnp.int32)
    tabs_qs, tabs_ks = _rope_tables(pos_s)
    qc_s, qr_s, kvc_s, kvs_s, kvw_s = _project(xs1, norm_g[1], wq_pad, wkv, None, tabs_qs, tabs_ks,
                                               prompt=False, seq=1)
    page_flat = page_table.reshape(-1).astype(jnp.int32)
    n_phys = cache_cmp_kv.shape[1]
    to_fm = lambda t: jnp.transpose(t, (0, 2, 3, 4, 1))
    lh_s = _cmp_lh_sample(to_fm(cache_cmp_kv[0]), page_flat, wc)
    nc_s = past // CMP_STRIDE
    nb_blk = past // SEL_BLOCK + 1
    nbp = -(-nb_blk // LANE) * LANE
    ov_np = np.zeros((nc_s, nbp), np.float32)
    ov_np[:, :nb_blk] = _overlap(nc_s, nb_blk)
    topi, oc_s = _sample_cmp(lh_s.reshape(nb, nc_s, lh_s.shape[1]), qc_s.reshape(nb, N_HEADS, SLOT),
                             pe, w1f, b1, w2p, b2p, jnp.asarray(ov_np, dtype=BF16), past=past)
    topi_flat = topi[:, :N_KV, :N_SELECT].reshape(-1)
    r_gate = xs1.shape[0]
    sz_s, gg_s = pl.pallas_call(
        _gate_kernel, grid=(1,),
        in_specs=[_const_spec(s, single=False) for s in
                  ((r_gate, d), (1, d), wz.shape, wgl.shape, gb.shape, e_mat.shape)],
        out_specs=(_const_spec((r_gate, N_BRANCH * qw), single=False),) * 2,
        out_shape=(jax.ShapeDtypeStruct((r_gate, N_BRANCH * qw), F32),) * 2,
        compiler_params=_params(("arbitrary",)), name="nsa_gate_sample",
    )(xs1, norm_g[1][None, :], wz, wgl, gb, e_mat)
    expand = lambda t: jnp.repeat(t.reshape(nb, N_KV, HEAD_DIM), Q_PER_KV, axis=1)
    y_heads, win_s_t = _sample_attn(
        page_flat, topi_flat, to_fm(cache_sel_kv[0]), qr_s.reshape(nb, N_HEADS, SLOT),
        expand(kvs_s[:, :KVW]), expand(kvs_s[:, KVW:]), expand(kvw_s[:, :KVW]), expand(kvw_s[:, KVW:]),
        to_fm(state_win_kv[0]), kvw_s.reshape(nb, 2, N_KV, HEAD_DIM, 1), oc_s,
        sz_s.reshape(nb, N_BRANCH, N_HEADS, HEAD_DIM), gg_s.reshape(nb, N_BRANCH, N_HEADS, HEAD_DIM),
        n_pages=n_pages)
    y_sample = pl.pallas_call(
        _out_kernel, grid=(1,),
        in_specs=[_const_spec(s, single=False) for s in ((nb, qw), (nb, d), wout.shape, (1, d))],
        out_specs=_const_spec((nb, d), single=False),
        out_shape=jax.ShapeDtypeStruct((nb, d), F32),
        compiler_params=_params(("arbitrary",)), name="nsa_out_sample",
    )(y_heads.reshape(nb, qw), xs1, wout, final_norm_g[None, :])

    kv6 = lambda t: t.reshape(1, nb, 1, 2, N_KV, HEAD_DIM)
    from_fm = lambda t: jnp.transpose(t, (0, 4, 1, 2, 3))[None]
    wl_p = min(WINDOW, seq)
    return (y_prompt.reshape(batch, seq, d), y_sample.reshape(nb, 1, d),
            from_fm(fm(kvc_t)), kv6(kvc_s), from_fm(fm(kvs_t)), kv6(kvs_s),
            from_fm(fm(kvw_t)[..., seq - wl_p:]), from_fm(win_s_t), chunk_v.reshape(1, nb, 1, -1))
```

```python
import functools
import math

import numpy as np
import jax
import jax.numpy as jnp
from jax import lax
from jax.experimental import pallas as pl
from jax.experimental.pallas import tpu as pltpu

F32 = jnp.float32
BF16 = jnp.bfloat16
EPS = 1e-6

A_CHUNK = 128
A_GROUPS = 16
N_HEADS = 16
HEAD_DIM = 64
N_KV = 4
Q_PER_KV = N_HEADS // N_KV
ROT_DIM = HEAD_DIM // 4
ROPE_THETA = 500000.0
CMP_BLOCK = 32
CMP_STRIDE = 16
SEL_BLOCK = 64
N_SELECT = 16
WINDOW = 512
N_BRANCH = 3
PAGE_SIZE = 128

LANE = 128
SLOT = LANE
KVW = N_KV * HEAD_DIM
KVROW = 2 * KVW
TM_ROWS = 256
TQ = 256
PAGES_PER_STEP = 32
VMEM_LIMIT = 56 * 1024 * 1024
MASK_BIG = float(2.0 ** 127)
SQRT_HALF = float(np.sqrt(0.5))
LOG2E = float(1.0 / np.log(2.0))
NEG_INF = float("-inf")


def _rms(x, g):
    return x * lax.rsqrt(jnp.mean(x * x, axis=-1, keepdims=True) + EPS) * g


def _gelu(x):
    return 0.5 * x * (1.0 + lax.erf(x * SQRT_HALF))


def _silu(x):
    return x * jax.nn.sigmoid(x)


def _dot(a, b):
    return jnp.dot(a, b, preferred_element_type=F32)


def _dot_nt(a, b):
    return lax.dot_general(a, b, (((1,), (1,)), ((), ())), preferred_element_type=F32)


def _split3(x):
    a = x.astype(BF16)
    r = x - a.astype(F32)
    b = r.astype(BF16)
    c = (r - b.astype(F32)).astype(BF16)
    return a, b, c


def _rope(a, c_tab, s1_tab, s2_tab):
    w = a.shape[-1]
    reps = w // LANE
    ct = jnp.concatenate([c_tab] * reps, axis=1) if reps > 1 else c_tab
    s1 = jnp.concatenate([s1_tab] * reps, axis=1) if reps > 1 else s1_tab
    s2 = jnp.concatenate([s2_tab] * reps, axis=1) if reps > 1 else s2_tab
    half = ROT_DIM // 2
    return a * ct + pltpu.roll(a, w - half, axis=1) * s1 + pltpu.roll(a, half, axis=1) * s2


def _const_spec(shape, single=True):
    nd = len(shape)
    kw = {"pipeline_mode": pl.Buffered(1)} if single else {}
    return pl.BlockSpec(tuple(shape), lambda *a, _nd=nd: (0,) * _nd, **kw)


def _params(sem):
    return pltpu.CompilerParams(dimension_semantics=sem, vmem_limit_bytes=VMEM_LIMIT)


def _layer_a_kernel(x_ref, g_ref, win_ref, lng_ref, lnb_ref, ws_ref, bs_ref, wout_ref, *rest, sample, tm, aw):
    if sample:
        xo_ref, v_ref = rest
    else:
        xo_ref, mix_ref = rest
    x = x_ref[...]
    xn = _rms(x, g_ref[...]).astype(BF16)
    v = _gelu(_dot(xn, win_ref[:, aw:2 * aw]))
    vc = v - jnp.mean(v, axis=-1, keepdims=True)
    v = vc * lax.rsqrt(jnp.mean(vc * vc, axis=-1, keepdims=True) + EPS) * lng_ref[...] + lnb_ref[...]
    if sample:
        v_ref[...] = v
        mix = v * ws_ref[...] + bs_ref[...]
    else:
        vb = v.astype(BF16)
        tri = (lax.broadcasted_iota(jnp.int32, (A_CHUNK, A_CHUNK), 0)
               >= lax.broadcasted_iota(jnp.int32, (A_CHUNK, A_CHUNK), 1))
        for g in range(A_GROUPS):
            wm = jnp.where(tri, ws_ref[g], 0.0).astype(BF16)
            bias = bs_ref[:, g:g + 1]
            for c in range(tm // A_CHUNK):
                blk = vb[c * A_CHUNK:(c + 1) * A_CHUNK, g * A_CHUNK:(g + 1) * A_CHUNK]
                mix_ref[c * A_CHUNK:(c + 1) * A_CHUNK, g * A_CHUNK:(g + 1) * A_CHUNK] = _dot(wm, blk) + bias
        mix = mix_ref[...]
    u = _gelu(_dot(xn, win_ref[:, 0:aw]))
    z = _dot(xn, win_ref[:, 2 * aw:3 * aw])
    y = (u * mix * _silu(z)).astype(BF16)
    xo_ref[...] = _dot(y, wout_ref[...]) + x


def _layer_a(x2d, norm_g, w_in, ln_g, ln_b, w_s, b_s, w_out, *, sample):
    r, d = x2d.shape
    aw = w_out.shape[0]
    tm = r if sample else TM_ROWS
    assert r % tm == 0 and tm % A_CHUNK == 0 or sample
    if sample:
        ws_arg = jnp.repeat(w_s[:, 0, 0], A_CHUNK)[None, :]
        bs_arg = jnp.repeat(b_s[:, 0], A_CHUNK)[None, :]
    else:
        ws_arg = w_s
        bs_arg = b_s.T
    row = pl.BlockSpec((tm, d), lambda i: (i, 0))
    in_specs = [row, _const_spec((1, d)), _const_spec(w_in.shape), _const_spec((1, aw)), _const_spec((1, aw)),
                _const_spec(ws_arg.shape), _const_spec(bs_arg.shape), _const_spec(w_out.shape)]
    if sample:
        out_shape = (jax.ShapeDtypeStruct((r, d), F32), jax.ShapeDtypeStruct((r, aw), F32))
        out_specs = (row, pl.BlockSpec((tm, aw), lambda i: (i, 0)))
        scratch = []
    else:
        out_shape = jax.ShapeDtypeStruct((r, d), F32)
        out_specs = row
        scratch = [pltpu.VMEM((tm, aw), F32)]
    return pl.pallas_call(
        functools.partial(_layer_a_kernel, sample=sample, tm=tm, aw=aw),
        grid=(r // tm,), in_specs=in_specs, out_specs=out_specs, out_shape=out_shape,
        scratch_shapes=scratch, compiler_params=_params(("parallel",)),
        name="layer_a_sample" if sample else "layer_a_prompt",
    )(x2d, norm_g[None, :], w_in, ln_g[None, :], ln_b[None, :], ws_arg, bs_arg, w_out)


def _proj_kernel(x_ref, g_ref, wq_ref, wkv_ref, *rest, prompt, tm, seq):
    if prompt:
        (wkp_ref, cq_ref, s1q_ref, s2q_ref, ck_ref, s1k_ref, s2k_ref,
         qc_ref, qr_ref, kvc_ref, kvs_ref, kvw_ref, ksa_ref, kwp_ref, vst_ref, vwt_ref) = rest
    else:
        (cq_ref, s1q_ref, s2q_ref, ck_ref, s1k_ref, s2k_ref,
         qc_ref, qr_ref, kvc_ref, kvs_ref, kvw_ref) = rest
    scale = HEAD_DIM ** -0.5
    xn = _rms(x_ref[...], g_ref[...]).astype(BF16)
    cq, s1q, s2q = cq_ref[...], s1q_ref[...], s2q_ref[...]
    ck, s1k, s2k = ck_ref[...], s1k_ref[...], s2k_ref[...]
    kv = _dot(xn, wkv_ref[...])
    kvc = kv[:, 0:KVROW]
    kvs = jnp.concatenate([_rope(kv[:, KVROW:KVROW + KVW], ck, s1k, s2k), kv[:, KVROW + KVW:2 * KVROW]], axis=1)
    kvw = jnp.concatenate([_rope(kv[:, 2 * KVROW:2 * KVROW + KVW], ck, s1k, s2k),
                           kv[:, 2 * KVROW + KVW:3 * KVROW]], axis=1)
    q = _dot(xn, wq_ref[...])
    if not prompt:
        qc_ref[...] = (q * scale).astype(BF16)
        qr_ref[...] = (_rope(q, cq, s1q, s2q) * scale).astype(BF16)
        kvc_ref[...] = kvc
        kvs_ref[...] = kvs
        kvw_ref[...] = kvw
        return
    qc_ref[...] = jnp.transpose(q * (scale * LOG2E)).astype(BF16)
    qr_ref[...] = jnp.transpose(_rope(q, ck, s1k, s2k) * (scale * LOG2E)).astype(BF16)
    kvc_ref[...] = jnp.transpose(kvc)
    kvs_t = jnp.transpose(kvs)
    kvw_t = jnp.transpose(kvw)
    kvs_ref[...] = kvs_t
    kvw_ref[...] = kvw_t
    ones_pad = jnp.where(lax.broadcasted_iota(jnp.int32, (SLOT - HEAD_DIM, tm), 0) == 0, 1.0, 0.0)
    for g in range(N_KV):
        r0 = KVW + g * HEAD_DIM
        vst_ref[g * SLOT:(g + 1) * SLOT, :] = jnp.concatenate([kvs_t[r0:r0 + HEAD_DIM], ones_pad], axis=0).astype(BF16)
        vwt_ref[g * SLOT:(g + 1) * SLOT, :] = jnp.concatenate([kvw_t[r0:r0 + HEAD_DIM], ones_pad], axis=0).astype(BF16)
    pw = N_KV * SLOT
    kp = _dot(xn, wkp_ref[...])
    lane = lax.broadcasted_iota(jnp.int32, (tm, pw), 1) % SLOT
    pos = (pl.program_id(0) % (seq // tm)) * tm + lax.broadcasted_iota(jnp.int32, (tm, pw), 0)
    onehot = jnp.where(pos // SEL_BLOCK == lane - HEAD_DIM, 1.0, 0.0)
    ksa_ref[...] = jnp.where(lane < HEAD_DIM, _rope(kp[:, 0:pw], cq, s1q, s2q), onehot).astype(BF16)
    kwp_ref[...] = _rope(kp[:, pw:2 * pw], cq, s1q, s2q).astype(BF16)


def _project(x2d, norm_g, wq, wkv, wk_pad, tabs_q, tabs_k, *, prompt, seq):
    r, d = x2d.shape
    tm = TQ if prompt else r
    nt = seq // tm if prompt else 1
    row = lambda w: pl.BlockSpec((tm, w), lambda i: (i, 0))
    tab = pl.BlockSpec((tm, LANE), (lambda i: (i % nt, 0)) if prompt else (lambda i: (0, 0)))
    pw = N_KV * SLOT
    in_specs = [row(d), _const_spec((1, d)), _const_spec(wq.shape), _const_spec(wkv.shape)]
    args = [x2d, norm_g[None, :], wq, wkv]
    if prompt:
        in_specs.append(_const_spec(wk_pad.shape))
        args.append(wk_pad)
    in_specs += [tab] * 6
    args += list(tabs_q) + list(tabs_k)
    if prompt:
        nbatch = r // seq
        qd = N_HEADS * HEAD_DIM
        tile_t = lambda rows: pl.BlockSpec((None, None, rows, tm), lambda i: (i // nt, i % nt, 0, 0))
        out_shape = [jax.ShapeDtypeStruct((nbatch, nt, qd, tm), BF16)] * 2
        out_specs = [tile_t(qd)] * 2
        out_shape += [jax.ShapeDtypeStruct((nbatch, KVROW, seq), F32)] * 3
        out_specs += [pl.BlockSpec((None, KVROW, tm), lambda i: (i // nt, 0, i % nt))] * 3
        out_shape += [jax.ShapeDtypeStruct((r, pw), BF16)] * 2
        out_specs += [row(pw)] * 2
        out_shape += [jax.ShapeDtypeStruct((nbatch, nt, pw, tm), BF16)] * 2
        out_specs += [tile_t(pw)] * 2
    else:
        qw = N_HEADS * SLOT
        out_shape = [jax.ShapeDtypeStruct((r, qw), BF16)] * 2
        out_specs = [row(qw), row(qw)]
        out_shape += [jax.ShapeDtypeStruct((r, KVROW), F32)] * 3
        out_specs += [row(KVROW)] * 3
    return pl.pallas_call(
        functools.partial(_proj_kernel, prompt=prompt, tm=tm, seq=seq),
        grid=(r // tm,), in_specs=in_specs, out_specs=tuple(out_specs), out_shape=tuple(out_shape),
        compiler_params=_params(("parallel",)),
        name="nsa_project_prompt" if prompt else "nsa_project_sample",
    )(*args)


def _cmp_lh_kernel(*refs, n_in, n_scalar):
    refs = refs[n_scalar:]
    x_refs = refs[:n_in]
    w_ref, lh_ref, xs_ref = refs[n_in], refs[n_in + 1], refs[n_in + 2]
    half = w_ref.shape[-1]
    tpb = x_refs[0].shape[-1]
    n_rows = lh_ref.shape[0]
    rows_u = n_rows // 2
    tiles_u = rows_u * CMP_STRIDE // LANE
    units = [(kv, hf) for kv in range(2) for hf in range(2)]

    pitch = xs_ref.shape[2] // CMP_STRIDE
    sub = 8

    def fill(u):
        kv, hf = units[u]
        for tt in range(tiles_u):
            tok0 = (hf * tiles_u + tt) * LANE
            r, t0 = x_refs[tok0 // tpb], tok0 % tpb
            for gp in range(N_KV // 2):
                tile = jnp.concatenate([r[kv, 2 * gp, :, t0:t0 + LANE], r[kv, 2 * gp + 1, :, t0:t0 + LANE]], axis=0)
                tile_t = jnp.transpose(tile)
                for v in range(LANE // sub):
                    j0 = (v * sub) % CMP_STRIDE
                    m = (tt * LANE + v * sub) // CMP_STRIDE
                    xs_ref[u, gp, pl.ds(j0 * pitch + m, sub, stride=pitch), :] = tile_t[v * sub:(v + 1) * sub, :]

    def matmuls(u):
        kv, hf = units[u]
        acc = None
        for j in range(CMP_STRIDE):
            xj = jnp.concatenate([xs_ref[u, gp, j * pitch:j * pitch + rows_u, :]
                                  for gp in range(N_KV // 2)], axis=1).astype(BF16)
            t = _dot(xj, w_ref[2 * j + kv])
            acc = t if acc is None else acc + t
        lh_ref[hf * rows_u:(hf + 1) * rows_u, kv * half:(kv + 1) * half] = acc

    fill(0)
    for u in range(len(units)):
        if u + 1 < len(units):
            fill(u + 1)
        matmuls(u)


def _cmp_lh_prompt(kvc_t, wc):
    nbatch, _, _, _, seq = kvc_t.shape
    tt = min(seq, 2048)
    assert seq % tt == 0 and tt % LANE == 0
    nt = seq // tt
    out_w = 2 * wc.shape[-1]
    return pl.pallas_call(
        functools.partial(_cmp_lh_kernel, n_in=1, n_scalar=0),
        grid=(nbatch, nt),
        in_specs=[pl.BlockSpec((None, 2, N_KV, HEAD_DIM, tt), lambda b, i: (b, 0, 0, 0, i)), _const_spec(wc.shape)],
        out_specs=pl.BlockSpec((tt // CMP_STRIDE, out_w), lambda b, i: (b * nt + i, 0)),
        out_shape=jax.ShapeDtypeStruct((nbatch * seq // CMP_STRIDE, out_w), F32),
        scratch_shapes=[pltpu.VMEM((4, N_KV // 2, CMP_STRIDE * (tt // (2 * CMP_STRIDE) + 4), LANE), F32)],
        compiler_params=_params(("parallel", "parallel")), name="cmp_lh_prompt",
    )(kvc_t, wc)


def _cmp_lh_sample(pool_t, page_flat, wc):
    n_pages = page_flat.shape[0]
    pps = PAGES_PER_STEP
    assert n_pages % pps == 0
    steps = n_pages // pps
    blk = (None,) + tuple(pool_t.shape[1:])
    page_specs = [pl.BlockSpec(blk, (lambda i, pt, _p=p: (pt[i * pps + _p], 0, 0, 0, 0))) for p in range(pps)]
    nd = wc.ndim
    w_spec = pl.BlockSpec(wc.shape, lambda i, pt: (0,) * nd, pipeline_mode=pl.Buffered(1))
    out_w = 2 * wc.shape[-1]
    rows = pps * PAGE_SIZE // CMP_STRIDE
    grid_spec = pltpu.PrefetchScalarGridSpec(
        num_scalar_prefetch=1, grid=(steps,),
        in_specs=page_specs + [w_spec],
        out_specs=pl.BlockSpec((rows, out_w), lambda i, pt: (i, 0)),
        scratch_shapes=[pltpu.VMEM((4, N_KV // 2, CMP_STRIDE * (rows // 2 + 4), LANE), F32)])
    return pl.pallas_call(
        functools.partial(_cmp_lh_kernel, n_in=pps, n_scalar=1),
        grid_spec=grid_spec,
        out_shape=jax.ShapeDtypeStruct((steps * rows, out_w), F32),
        compiler_params=_params(("parallel",)), name="cmp_lh_sample",
    )(page_flat, *([pool_t] * pps), wc)


def _compress_finish(lh, pe_ref, w1f_ref, b1_ref, w2_ref, b2_ref, f32_v=False):
    nc = lh.shape[0]
    outs = []
    for kv in range(2):
        pe8 = jnp.broadcast_to(pe_ref[kv], (8, pe_ref.shape[-1])).astype(BF16)
        cb = _dot(pe8, w1f_ref[kv])[0:1, :] + b1_ref[kv]
        lo = lh[:, kv * 2 * KVW:kv * 2 * KVW + KVW]
        hi = lh[:, kv * 2 * KVW + KVW:(kv + 1) * 2 * KVW]
        hi_next = pltpu.roll(hi, nc - 1, axis=0)
        hid = _silu(lo + hi_next + cb)
        out = _dot(hid.astype(BF16), w2_ref[kv]) + b2_ref[kv]
        outs.append(out if (f32_v and kv == 1) else out.astype(BF16))
    return outs


def _select_blocks(imp, blk, jq, axis):
    forced = (blk == 0) | (blk == jq) | (blk == jq - 1)
    score = jnp.where((blk > jq) | (blk < 0), NEG_INF, jnp.where(forced, jnp.inf, imp))
    blk_f = blk.astype(F32)
    sel = jnp.zeros(imp.shape, F32)
    idxs = []
    for _ in range(N_SELECT):
        m = jnp.max(score, axis=axis, keepdims=True)
        idx = jnp.min(jnp.where(score == m, blk_f, 1e9), axis=axis, keepdims=True)
        hit = (blk_f == idx) & (m > NEG_INF)
        sel = jnp.where(hit, 1.0, sel)
        score = jnp.where(blk_f == idx, NEG_INF, score)
        idxs.append(idx)
    return sel, idxs


def _gate(xn, wz_ref, wgl_ref, gb_ref, e_ref):
    z = _dot(xn, wz_ref[...])
    gl = _dot(xn, wgl_ref[...]) + gb_ref[...]
    e = e_ref[...]
    a, b, c = _split3(gl)
    gle = _dot(a, e) + _dot(b, e) + _dot(c, e)
    return _silu(z), jax.nn.sigmoid(gle)


def _gate_kernel(x_ref, g_ref, wz_ref, wgl_ref, gb_ref, e_ref, sz_ref, gg_ref):
    xn = _rms(x_ref[...], g_ref[...]).astype(BF16)
    sz, gg = _gate(xn, wz_ref, wgl_ref, gb_ref, e_ref)
    sz_ref[...] = sz
    gg_ref[...] = gg


def _out_kernel(y_ref, x_ref, w_ref, g_ref, o_ref):
    o_ref[...] = _rms(_dot(y_ref[...].astype(BF16), w_ref[...]) + x_ref[...], g_ref[...])


def _attn_prompt_kernel(x_ref, qc_ref, qr_ref, lh_ref, ksa_ref, vsp_ref,
                        kw0_ref, kw1_ref, kw2_ref, vw0_ref, vw1_ref, vw2_ref,
                        g1_ref, gf_ref, wz_ref, wgl_ref, gb_ref, e_ref, wout_ref,
                        pe_ref, w1f_ref, b1_ref, w2_ref, b2_ref, ovt_ref,
                        out_ref, kc_scr, vc_scr, o_scr, *, tq, seq):
    qi = pl.program_id(1)
    t0 = qi * tq
    nc = seq // CMP_STRIDE
    rows = Q_PER_KV * tq

    @pl.when(qi == 0)
    def _():
        kc, vc = _compress_finish(lh_ref[...], pe_ref, w1f_ref, b1_ref, w2_ref, b2_ref)
        kc_scr[...] = kc
        vc_scr[...] = vc

    a_row = lax.broadcasted_iota(jnp.int32, (rows, 1), 0) % tq
    pos_row = t0 + a_row
    c_col = lax.broadcasted_iota(jnp.int32, (1, tq), 1)
    n_col = lax.broadcasted_iota(jnp.int32, (1, nc), 1)
    mask_c = (n_col * CMP_STRIDE + CMP_BLOCK - 1) <= pos_row
    causal = c_col <= a_row
    blk_t = lax.broadcasted_iota(jnp.int32, (2 * SEL_BLOCK, tq), 0) - SEL_BLOCK
    jq_t = (t0 + lax.broadcasted_iota(jnp.int32, (2 * SEL_BLOCK, tq), 1)) // SEL_BLOCK
    lane_q = lax.broadcasted_iota(jnp.int32, (tq, SLOT), 1)
    tiny = jnp.finfo(jnp.float32).tiny

    def stack(ref, g):
        return jnp.concatenate([ref[:, (Q_PER_KV * g + r) * SLOT:(Q_PER_KV * g + r + 1) * SLOT]
                                for r in range(Q_PER_KV)], axis=0)

    def put(br, g, o):
        for j in range(Q_PER_KV // 2):
            pair = jnp.concatenate([o[(2 * j) * tq:(2 * j + 1) * tq, 0:HEAD_DIM],
                                    o[(2 * j + 1) * tq:(2 * j + 2) * tq, 0:HEAD_DIM]], axis=1)
            h0 = Q_PER_KV * g + 2 * j
            o_scr[br, :, h0 * HEAD_DIM:(h0 + 2) * HEAD_DIM] = pair

    for g in range(N_KV):
        gs = slice(g * SLOT, (g + 1) * SLOT)
        s = _dot_nt(stack(qc_ref, g), kc_scr[:, gs])
        s = jnp.where(mask_c, s, NEG_INF)
        m = jnp.max(s, axis=-1, keepdims=True)
        m = jnp.where(m > NEG_INF, m, 0.0)
        e = jnp.where(mask_c, jnp.exp(s - m), 0.0)
        p = e / jnp.maximum(jnp.sum(e, axis=-1, keepdims=True), tiny)
        put(0, g, _dot(p.astype(BF16), vc_scr[:, gs]))
        psum = p[0:tq] + p[tq:2 * tq] + p[2 * tq:3 * tq] + p[3 * tq:4 * tq]
        ovt = ovt_ref[...]
        pa, pb, pc = _split3(psum)
        imp_t = _dot_nt(ovt, pa) + _dot_nt(ovt, pb) + _dot_nt(ovt, pc)
        sel_t, _ = _select_blocks(imp_t, blk_t, jq_t, axis=0)
        sel = jnp.transpose(sel_t)
        maskpart = jnp.where(lane_q >= HEAD_DIM, (sel - 1.0) * MASK_BIG, 0.0).astype(BF16)
        qr_g = stack(qr_ref, g)
        qa = jnp.where(jnp.concatenate([lane_q] * Q_PER_KV, axis=0) < HEAD_DIM, qr_g,
                       jnp.concatenate([maskpart] * Q_PER_KV, axis=0))

        def sel_tile(kt, carry, diag):
            m_i, l_i, acc = carry
            start = pl.multiple_of(kt * tq, tq)
            k_t = ksa_ref[kt, g * SLOT:(g + 1) * SLOT, :]
            v = vsp_ref[pl.ds(start, tq), gs]
            st = _dot(qa, k_t)
            if diag:
                st = jnp.where(causal, st, -MASK_BIG)
            m_new = jnp.maximum(m_i, jnp.max(st, axis=-1, keepdims=True))
            alpha = jnp.exp(m_i - m_new)
            pt = jnp.exp(st - m_new)
            l_new = alpha * l_i + jnp.sum(pt, axis=-1, keepdims=True)
            acc_new = alpha * acc + _dot(pt.astype(BF16), v)
            return m_new, l_new, acc_new

        init = (jnp.full((rows, 1), NEG_INF, F32), jnp.zeros((rows, 1), F32), jnp.zeros((rows, SLOT), F32))
        carry = lax.fori_loop(0, qi, lambda kt, c: sel_tile(kt, c, False), init)
        _, l_s, acc_s = sel_tile(qi, carry, True)
        put(1, g, acc_s / l_s)
        s0 = _dot(qr_g, kw0_ref[gs, :])
        s1 = _dot(qr_g, kw1_ref[gs, :])
        s2 = _dot(qr_g, kw2_ref[gs, :])
        s0 = jnp.where((c_col > a_row) & (qi >= 2), s0, NEG_INF)
        s1 = jnp.where(qi >= 1, s1, NEG_INF)
        s2 = jnp.where(causal, s2, NEG_INF)
        mw = jnp.maximum(jnp.maximum(jnp.max(s0, axis=-1, keepdims=True), jnp.max(s1, axis=-1, keepdims=True)),
                         jnp.max(s2, axis=-1, keepdims=True))
        p0, p1, p2 = jnp.exp(s0 - mw), jnp.exp(s1 - mw), jnp.exp(s2 - mw)
        lw = (jnp.sum(p0, axis=-1, keepdims=True) + jnp.sum(p1, axis=-1, keepdims=True)
              + jnp.sum(p2, axis=-1, keepdims=True))
        ow = (_dot(p0.astype(BF16), vw0_ref[:, gs]) + _dot(p1.astype(BF16), vw1_ref[:, gs])
              + _dot(p2.astype(BF16), vw2_ref[:, gs]))
        put(2, g, ow / lw)

    x = x_ref[...]
    xn = _rms(x, g1_ref[...]).astype(BF16)
    sz, gg = _gate(xn, wz_ref, wgl_ref, gb_ref, e_ref)
    qw = N_HEADS * HEAD_DIM
    y = None
    for br in range(N_BRANCH):
        t = o_scr[br] * sz[:, br * qw:(br + 1) * qw] * gg[:, br * qw:(br + 1) * qw]
        y = t if y is None else y + t
    out_ref[...] = _rms(_dot(y.astype(BF16), wout_ref[...]) + x, gf_ref[...])


def _attn_prompt(x1, qc, qr, lh, ksa, vsp, kwp, vwp, g1, gf, wz, wgl, gb, e_mat, wout,
                 pe, w1f, b1, w2p, b2p, ovt, *, batch, seq):
    tq = TQ
    assert seq % tq == 0 and WINDOW == 2 * tq and seq % SEL_BLOCK == 0 and seq // SEL_BLOCK <= SEL_BLOCK
    nq = seq // tq
    nc = seq // CMP_STRIDE
    d = x1.shape[1]
    qw = N_HEADS * SLOT
    pw = N_KV * SLOT
    tile = lambda w: pl.BlockSpec((tq, w), lambda b, i: (b * nq + i, 0))
    per_b = lambda rows, w: pl.BlockSpec((rows, w), lambda b, i: (b, 0), pipeline_mode=pl.Buffered(1))
    ksa_spec = pl.BlockSpec((None, nq, pw, tq), lambda b, i: (b, 0, 0, 0), pipeline_mode=pl.Buffered(1))
    kwin = lambda off: pl.BlockSpec((None, None, pw, tq), lambda b, i: (b, jnp.maximum(i - off, 0), 0, 0))
    win = lambda off: pl.BlockSpec((tq, pw), lambda b, i: (b * nq + jnp.maximum(i - off, 0), 0))
    consts = [g1[None, :], gf[None, :], wz, wgl, gb, e_mat, wout, pe, w1f, b1, w2p, b2p, ovt]
    in_specs = ([tile(d), tile(qw), tile(qw), per_b(nc, lh.shape[1]), ksa_spec, per_b(seq, pw),
                 kwin(2), kwin(1), kwin(0), win(2), win(1), win(0)]
                + [_const_spec(c.shape) for c in consts])
    return pl.pallas_call(
        functools.partial(_attn_prompt_kernel, tq=tq, seq=seq),
        grid=(batch, nq), in_specs=in_specs, out_specs=tile(d),
        out_shape=jax.ShapeDtypeStruct((batch * seq, d), F32),
        scratch_shapes=[pltpu.VMEM((nc, pw), BF16), pltpu.VMEM((nc, pw), BF16),
                        pltpu.VMEM((N_BRANCH, tq, N_HEADS * HEAD_DIM), F32)],
        compiler_params=_params(("arbitrary", "arbitrary")), name="nsa_attend_prompt",
    )(x1, qc, qr, lh, ksa, vsp, kwp, kwp, kwp, vwp, vwp, vwp, *consts)


def _attn_t_kernel(x_ref, qc_ref, qr_ref, lh_ref, ksa_ref, vst_ref,
                   kw0_ref, kw1_ref, kw2_ref, vw0_ref, vw1_ref, vw2_ref,
                   g1_ref, gf_ref, wz_ref, wgl_ref, gb_ref, wout_ref,
                   pe_ref, w1f_ref, b1_ref, w2_ref, b2_ref, ov_ref,
                   out_ref, kc_scr, vct_scr, o_scr, *, tq, seq):
    qi = pl.program_id(1)
    t0 = qi * tq
    nc = seq // CMP_STRIDE
    cols = Q_PER_KV * tq
    hd = HEAD_DIM

    @pl.when(qi == 0)
    def _():
        kc, vc = _compress_finish(lh_ref[...], pe_ref, w1f_ref, b1_ref, w2_ref, b2_ref, f32_v=True)
        kc_scr[...] = kc
        vct_scr[...] = jnp.transpose(vc).astype(BF16)

    a_col = lax.broadcasted_iota(jnp.int32, (1, cols), 1) % tq
    pos_col = t0 + a_col
    c_row = lax.broadcasted_iota(jnp.int32, (tq, 1), 0)
    n_row = lax.broadcasted_iota(jnp.int32, (nc, 1), 0)
    mask_c = (n_row * CMP_STRIDE + CMP_BLOCK - 1) <= pos_col
    causal = c_row <= a_col
    blk_t = lax.broadcasted_iota(jnp.int32, (SEL_BLOCK, tq), 0)
    jq_t = (t0 + lax.broadcasted_iota(jnp.int32, (SEL_BLOCK, tq), 1)) // SEL_BLOCK
    tiny = jnp.finfo(jnp.float32).tiny
    zpad = jnp.zeros((SLOT - hd, tq), BF16)

    def stack_t(ref, g, lower):
        return jnp.concatenate(
            [jnp.concatenate([ref[(Q_PER_KV * g + r) * hd:(Q_PER_KV * g + r + 1) * hd, :], lower], axis=0)
             for r in range(Q_PER_KV)], axis=1)

    def put(br, g, o_t):
        for r in range(Q_PER_KV):
            h = Q_PER_KV * g + r
            o_scr[br, h * hd:(h + 1) * hd, :] = o_t[:, r * tq:(r + 1) * tq]

    gsl = lambda g: slice(g * SLOT, (g + 1) * SLOT)

    def cmp_scores(g):
        return _dot(kc_scr[:, gsl(g)], stack_t(qc_ref, g, zpad))

    def cmp_finish(g, s):
        s = jnp.where(mask_c, s, NEG_INF)
        m = jnp.max(s, axis=0, keepdims=True)
        m = jnp.where(m > NEG_INF, m, 0.0)
        e = jnp.exp2(s - m)
        p = e * (1.0 / jnp.maximum(jnp.sum(e, axis=0, keepdims=True), tiny))
        put(0, g, _dot(vct_scr[gsl(g), :], p.astype(BF16))[0:hd])
        psum = p[:, 0:tq] + p[:, tq:2 * tq] + p[:, 2 * tq:3 * tq] + p[:, 3 * tq:4 * tq]
        ov = ov_ref[...]
        pa, pb, pc = _split3(psum)
        return _dot(ov, pa) + _dot(ov, pb) + _dot(ov, pc)

    def win_scores(g):
        qr_t = stack_t(qr_ref, g, zpad)
        return (_dot(kw0_ref[:, gsl(g)], qr_t), _dot(kw1_ref[:, gsl(g)], qr_t), _dot(kw2_ref[:, gsl(g)], qr_t))

    def win_finish(g, ss):
        s0 = jnp.where((c_row > a_col) & (qi >= 2), ss[0], NEG_INF)
        s1 = jnp.where(qi >= 1, ss[1], NEG_INF)
        s2 = jnp.where(causal, ss[2], NEG_INF)
        mw = jnp.maximum(jnp.maximum(jnp.max(s0, axis=0, keepdims=True), jnp.max(s1, axis=0, keepdims=True)),
                         jnp.max(s2, axis=0, keepdims=True))
        acc_w = (_dot(vw0_ref[gsl(g), :], jnp.exp2(s0 - mw).astype(BF16))
                 + _dot(vw1_ref[gsl(g), :], jnp.exp2(s1 - mw).astype(BF16))
                 + _dot(vw2_ref[gsl(g), :], jnp.exp2(s2 - mw).astype(BF16)))
        put(2, g, acc_w[0:hd] * (1.0 / acc_w[hd:hd + 1]))

    imps = [None] * N_KV
    s_c = cmp_scores(0)
    for g in range(N_KV):
        s_next = cmp_scores(g + 1) if g + 1 < N_KV else win_scores(0)
        imps[g] = cmp_finish(g, s_c)
        s_c = s_next
    sel_all, _ = _select_blocks(jnp.concatenate(imps, axis=1), jnp.concatenate([blk_t] * N_KV, axis=1),
                                jnp.concatenate([jq_t] * N_KV, axis=1), axis=0)
    maskbias_all = ((sel_all - 1.0) * MASK_BIG).astype(BF16)
    s_w = s_c
    for g in range(N_KV):
        s_next = win_scores(g + 1) if g + 1 < N_KV else None
        win_finish(g, s_w)
        s_w = s_next

    qa = [stack_t(qr_ref, g, maskbias_all[:, g * tq:(g + 1) * tq]) for g in range(N_KV)]

    def scores(g, kt):
        start = pl.multiple_of(kt * tq, tq)
        return _dot(ksa_ref[pl.ds(start, tq), gsl(g)], qa[g])

    def consume(g, kt, st, m_i, acc):
        m_new = jnp.maximum(m_i, jnp.max(st, axis=0, keepdims=True))
        alpha = jnp.exp2(m_i - m_new)
        pt = jnp.exp2(st - m_new).astype(BF16)
        return m_new, alpha * acc + _dot(vst_ref[kt, gsl(g), :], pt)

    def all_groups(kt, carry, diag):
        out = []
        st = scores(0, kt)
        for g in range(N_KV):
            st_next = scores(g + 1, kt) if g + 1 < N_KV else None
            if diag:
                st = jnp.where(causal, st, -MASK_BIG)
            out.extend(consume(g, kt, st, carry[2 * g], carry[2 * g + 1]))
            st = st_next
        return tuple(out)

    init = (jnp.full((1, cols), NEG_INF, F32), jnp.zeros((SLOT, cols), F32)) * N_KV
    carry = lax.fori_loop(0, qi, lambda kt, c: all_groups(kt, c, False), init)
    carry = all_groups(qi, carry, True)
    for g in range(N_KV):
        acc_s = carry[2 * g + 1]
        put(1, g, acc_s[0:hd] * (1.0 / acc_s[hd:hd + 1]))

    x = x_ref[...]
    xn_t = jnp.transpose(_rms(x, g1_ref[...])).astype(BF16)
    gate = jax.nn.sigmoid(_dot(wgl_ref[...], xn_t) + jnp.concatenate([gb_ref[...]] * (tq // LANE), axis=1))
    qw = N_HEADS * hd
    y_t = None
    for br in range(N_BRANCH):
        sz = _silu(_dot(wz_ref[br * qw:(br + 1) * qw, :], xn_t))
        parts = []
        for h in range(N_HEADS):
            j = br * N_HEADS + h
            parts.append(o_scr[br, h * hd:(h + 1) * hd, :] * sz[h * hd:(h + 1) * hd, :] * gate[j:j + 1, :])
        t = jnp.concatenate(parts, axis=0)
        y_t = t if y_t is None else y_t + t
    o = jnp.transpose(_dot(wout_ref[...], y_t.astype(BF16)))
    out_ref[...] = _rms(o + x, gf_ref[...])


def _attn_prompt_t(x1, qc_t, qr_t, lh, ksa, vst, kwp, vwt, g1, gf, wz_t, wgl_t, gb_b, wout_t,
                   pe, w1f, b1, w2p, b2p, ov, *, batch, seq):
    tq = TQ
    assert seq % tq == 0 and WINDOW == 2 * tq and seq % SEL_BLOCK == 0 and seq // SEL_BLOCK <= SEL_BLOCK
    nq = seq // tq
    nc = seq // CMP_STRIDE
    d = x1.shape[1]
    pw = N_KV * SLOT
    qd = N_HEADS * HEAD_DIM
    tile = lambda w: pl.BlockSpec((tq, w), lambda b, i: (b * nq + i, 0))
    tile_t = lambda rows: pl.BlockSpec((None, None, rows, tq), lambda b, i: (b, i, 0, 0))
    per_b = lambda rows, w: pl.BlockSpec((rows, w), lambda b, i: (b, 0), pipeline_mode=pl.Buffered(1))
    vst_spec = pl.BlockSpec((None, nq, pw, tq), lambda b, i: (b, 0, 0, 0), pipeline_mode=pl.Buffered(1))
    kwin = lambda off: pl.BlockSpec((tq, pw), lambda b, i: (b * nq + jnp.maximum(i - off, 0), 0))
    vwin = lambda off: pl.BlockSpec((None, None, pw, tq), lambda b, i: (b, jnp.maximum(i - off, 0), 0, 0))
    consts = [g1[None, :], gf[None, :], wz_t, wgl_t, gb_b, wout_t, pe, w1f, b1, w2p, b2p, ov]
    in_specs = ([tile(d), tile_t(qd), tile_t(qd), per_b(nc, lh.shape[1]), per_b(seq, pw), vst_spec,
                 kwin(2), kwin(1), kwin(0), vwin(2), vwin(1), vwin(0)]
                + [_const_spec(c.shape) for c in consts])
    return pl.pallas_call(
        functools.partial(_attn_t_kernel, tq=tq, seq=seq),
        grid=(batch, nq), in_specs=in_specs, out_specs=tile(d),
        out_shape=jax.ShapeDtypeStruct((batch * seq, d), F32),
        scratch_shapes=[pltpu.VMEM((nc, pw), BF16), pltpu.VMEM((pw, nc), BF16),
                        pltpu.VMEM((N_BRANCH, qd, tq), F32)],
        compiler_params=_params(("arbitrary", "arbitrary")), name="nsa_attend_prompt",
    )(x1, qc_t, qr_t, lh, ksa, vst, kwp, kwp, kwp, vwt, vwt, vwt, *consts)


def _sample_cmp_kernel(lh_ref, qc_ref, pe_ref, w1f_ref, b1_ref, w2_ref, b2_ref, ov_ref,
                       imp_ref, oc_ref, *, past):
    kc, vc = _compress_finish(lh_ref[...], pe_ref, w1f_ref, b1_ref, w2_ref, b2_ref)
    nc = kc.shape[0]
    nbp = ov_ref.shape[1]
    q = qc_ref[...]
    head_row = lax.broadcasted_iota(jnp.int32, (N_HEADS, 1), 0)
    n_col = lax.broadcasted_iota(jnp.int32, (1, nc), 1)
    mask_c = (n_col * CMP_STRIDE + CMP_BLOCK - 1) <= past
    row8 = lax.broadcasted_iota(jnp.int32, (8, 1), 0)
    tiny = jnp.finfo(jnp.float32).tiny
    oc = jnp.zeros((N_HEADS, SLOT), F32)
    psum8 = jnp.zeros((8, nc), F32)
    for g in range(N_KV):
        gs = slice(g * SLOT, (g + 1) * SLOT)
        in_g = (head_row // Q_PER_KV) == g
        s = jnp.where(mask_c, _dot_nt(q, kc[:, gs]), NEG_INF)
        m = jnp.max(s, axis=-1, keepdims=True)
        m = jnp.where(m > NEG_INF, m, 0.0)
        e = jnp.where(mask_c, jnp.exp(s - m), 0.0)
        p = e / jnp.maximum(jnp.sum(e, axis=-1, keepdims=True), tiny)
        oc = jnp.where(in_g, _dot(p.astype(BF16), vc[:, gs]), oc)
        pg = jnp.sum(jnp.where(in_g, p, 0.0), axis=0, keepdims=True)
        psum8 = jnp.where(row8 == g, pg, psum8)
    ov = ov_ref[...]
    pa, pb, pc = _split3(psum8)
    imp_ref[...] = _dot(pa, ov) + _dot(pb, ov) + _dot(pc, ov)
    oc_ref[...] = oc


def _sample_topk_kernel(imp_ref, topi_ref, *, past):
    imp_t = jnp.transpose(imp_ref[...])
    blk = lax.broadcasted_iota(jnp.int32, imp_t.shape, 0)
    blk = jnp.where(blk < past // SEL_BLOCK + 1, blk, -1)
    _, idxs = _select_blocks(imp_t, blk, jnp.int32(past // SEL_BLOCK), axis=0)
    topi_ref[...] = jnp.concatenate(idxs, axis=0).astype(jnp.int32)


def _sample_cmp(lh3, qc3, pe, w1f, b1, w2p, b2p, ov, *, past):
    nb, nc, w = lh3.shape
    nbp = ov.shape[1]
    consts = [pe, w1f, b1, w2p, b2p, ov]
    imp, oc = pl.pallas_call(
        functools.partial(_sample_cmp_kernel, past=past),
        grid=(nb,),
        in_specs=[pl.BlockSpec((None, nc, w), lambda b: (b, 0, 0)),
                  pl.BlockSpec((None, N_HEADS, SLOT), lambda b: (b, 0, 0))]
                 + [_const_spec(c.shape) for c in consts],
        out_specs=(pl.BlockSpec((None, 8, nbp), lambda b: (b, 0, 0)),
                   pl.BlockSpec((None, N_HEADS, SLOT), lambda b: (b, 0, 0))),
        out_shape=(jax.ShapeDtypeStruct((nb, 8, nbp), F32),
                   jax.ShapeDtypeStruct((nb, N_HEADS, SLOT), F32)),
        compiler_params=_params(("parallel",)), name="nsa_sample_cmp",
    )(lh3, qc3, *consts)
    rows = nb * 8
    topi = pl.pallas_call(
        functools.partial(_sample_topk_kernel, past=past), grid=(1,),
        in_specs=[_const_spec((rows, nbp), single=False)],
        out_specs=_const_spec((N_SELECT, rows), single=False),
        out_shape=jax.ShapeDtypeStruct((N_SELECT, rows), jnp.int32),
        compiler_params=_params(("arbitrary",)), name="nsa_sample_topk",
    )(imp.reshape(rows, nbp))
    topi = jnp.transpose(topi.reshape(N_SELECT, nb, 8)[:, :, :N_KV], (1, 2, 0))
    return topi, oc


def _sample_attn_kernel(pt_ref, ti_ref, *refs, n_pool_blk, wlen):
    kv_refs = refs[:N_SELECT]
    (qr_ref, ksn_ref, vsn_ref, kwn_ref, vwn_ref, win_ref, wnew_ref, oc_ref, sz_ref, gg_ref,
     y_ref, wout_ref) = refs[N_SELECT:]
    b = pl.program_id(0)
    g = pl.program_id(1)
    per_page = PAGE_SIZE // SEL_BLOCK
    q = qr_ref[...]
    qf = q[:, 0:HEAD_DIM].astype(F32)
    in_g = (lax.broadcasted_iota(jnp.int32, (N_HEADS, 1), 0) // Q_PER_KV) == g
    zpad = jnp.zeros((SLOT - HEAD_DIM, PAGE_SIZE), BF16)
    tok_half = lax.broadcasted_iota(jnp.int32, (1, PAGE_SIZE), 1) // SEL_BLOCK

    s_new = jnp.sum(qf * ksn_ref[...], axis=-1, keepdims=True)
    scores = []
    m = s_new
    for r in range(N_SELECT):
        t = ti_ref[(b * N_KV + g) * N_SELECT + r]
        ok = (tok_half == t % per_page) & (t < n_pool_blk)
        k_t = jnp.concatenate([kv_refs[r][0].astype(BF16), zpad], axis=0)
        s = jnp.where(ok, _dot(q, k_t), NEG_INF)
        scores.append(s)
        m = jnp.maximum(m, jnp.max(s, axis=-1, keepdims=True))
    p_new = jnp.exp(s_new - m)
    l = p_new
    o = p_new * vsn_ref[...]
    for r in range(N_SELECT):
        p = jnp.exp(scores[r] - m)
        l = l + jnp.sum(p, axis=-1, keepdims=True)
        o = o + _dot_nt(p.astype(BF16), kv_refs[r][1].astype(BF16))
    o_s = o / l

    wkey = lax.broadcasted_iota(jnp.int32, (1, wlen), 1)
    kw_t = jnp.concatenate([win_ref[0].astype(BF16), jnp.zeros((SLOT - HEAD_DIM, wlen), BF16)], axis=0)
    s = jnp.where(wkey > wlen - WINDOW, _dot(q, kw_t), NEG_INF)
    s_new = jnp.sum(qf * kwn_ref[...], axis=-1, keepdims=True)
    m = jnp.maximum(jnp.max(s, axis=-1, keepdims=True), s_new)
    p = jnp.exp(s - m)
    p_new = jnp.exp(s_new - m)
    l = jnp.sum(p, axis=-1, keepdims=True) + p_new
    o_w = (_dot_nt(p.astype(BF16), win_ref[1].astype(BF16)) + p_new * vwn_ref[...]) / l

    o_c = oc_ref[:, 0:HEAD_DIM]
    y = o_c * sz_ref[0] * gg_ref[0] + o_s * sz_ref[1] * gg_ref[1] + o_w * sz_ref[2] * gg_ref[2]

    @pl.when(g == 0)
    def _():
        y_ref[...] = jnp.zeros(y_ref.shape, F32)

    y_ref[...] = jnp.where(in_g, y, y_ref[...])
    for kv in range(2):
        shifted = pltpu.roll(win_ref[kv], wlen - 1, axis=1)
        wout_ref[kv] = jnp.where(wkey == wlen - 1, wnew_ref[kv], shifted)


def _sample_attn(page_flat, topi_flat, pool_t, qr3, ksn, vsn, kwn, vwn, win_t, wnew, oc, sz4, gg4, *, n_pages):
    nb = qr3.shape[0]
    wlen = win_t.shape[-1]
    per_page = PAGE_SIZE // SEL_BLOCK
    n_pool_blk = n_pages * per_page

    def pool_map(r):
        def f(b, g, pt, ti):
            blk = jnp.minimum(ti[(b * N_KV + g) * N_SELECT + r], n_pool_blk - 1)
            return (pt[b * n_pages + blk // per_page], 0, g, 0, 0)
        return f

    b3 = lambda s1, s2: pl.BlockSpec((None, s1, s2), lambda b, g, pt, ti: (b, 0, 0))
    b4 = pl.BlockSpec((None, N_BRANCH, N_HEADS, HEAD_DIM), lambda b, g, pt, ti: (b, 0, 0, 0))
    page = lambda r: pl.BlockSpec((None, 2, None, HEAD_DIM, PAGE_SIZE), pool_map(r))
    wspec = lambda last: pl.BlockSpec((None, 2, None, HEAD_DIM, last), lambda b, g, pt, ti: (b, 0, g, 0, 0))
    in_specs = ([page(r) for r in range(N_SELECT)]
                + [b3(N_HEADS, SLOT)] + [b3(N_HEADS, HEAD_DIM)] * 4
                + [wspec(wlen), wspec(1), b3(N_HEADS, SLOT), b4, b4])
    grid_spec = pltpu.PrefetchScalarGridSpec(
        num_scalar_prefetch=2, grid=(nb, N_KV), in_specs=in_specs,
        out_specs=(b3(N_HEADS, HEAD_DIM), wspec(wlen)))
    return pl.pallas_call(
        functools.partial(_sample_attn_kernel, n_pool_blk=n_pool_blk, wlen=wlen),
        grid_spec=grid_spec,
        out_shape=(jax.ShapeDtypeStruct((nb, N_HEADS, HEAD_DIM), F32),
                   jax.ShapeDtypeStruct(win_t.shape, F32)),
        compiler_params=_params(("arbitrary", "arbitrary")), name="nsa_sample_sel_win",
    )(page_flat, topi_flat, *([pool_t] * N_SELECT), qr3, ksn, vsn, kwn, vwn, win_t, wnew, oc, sz4, gg4)


def _rope_tables(pos):
    half = ROT_DIM // 2
    freqs = jnp.exp(-math.log(ROPE_THETA) * jnp.arange(half, dtype=F32) * (2.0 / ROT_DIM))
    ang = pos.astype(F32)[:, None] * freqs[None, :]
    cos, sin = jnp.cos(ang), jnp.sin(ang)
    n = pos.shape[0]
    one = jnp.ones((n, HEAD_DIM - ROT_DIM), F32)
    zero8 = jnp.zeros((n, half), F32)
    zrest = jnp.zeros((n, HEAD_DIM - ROT_DIM), F32)
    c64 = jnp.concatenate([cos, cos, one], axis=1)
    s1_64 = jnp.concatenate([-sin, zero8, zrest], axis=1)
    s2_64 = jnp.concatenate([zero8, sin, zrest], axis=1)
    pad1 = jnp.ones((n, SLOT - HEAD_DIM), F32)
    pad0 = jnp.zeros((n, SLOT - HEAD_DIM), F32)
    tabs_q = (jnp.concatenate([c64, pad1], axis=1), jnp.concatenate([s1_64, pad0], axis=1),
              jnp.concatenate([s2_64, pad0], axis=1))
    tabs_k = tuple(jnp.concatenate([t, t], axis=1) for t in (c64, s1_64, s2_64))
    return tabs_q, tabs_k


def _pad_heads(w, n_heads):
    k = w.shape[0]
    w3 = w.reshape(k, n_heads, HEAD_DIM)
    return jnp.pad(w3, ((0, 0), (0, 0), (0, SLOT - HEAD_DIM))).reshape(k, n_heads * SLOT)


def _overlap(nc, nb):
    n = np.arange(nc)[:, None] * CMP_STRIDE
    j = np.arange(nb)[None, :] * SEL_BLOCK
    return ((n <= j + SEL_BLOCK - 1) & (n + CMP_BLOCK - 1 >= j)).astype(np.float32)


def kernel(x_prompt, x_sample, cache_cmp_kv, cache_sel_kv, state_win_kv, page_table, norm_g, final_norm_g,
           a_w_in, a_ln_g, a_ln_b, a_w_s, a_b_s, a_w_out, b_w_in, b_cmp_pe, b_cmp_w1, b_cmp_b1, b_cmp_w2,
           b_cmp_b2, b_gate_b, b_w_out):
    batch, seq, d = x_prompt.shape
    nb, dec_seq, _ = x_sample.shape
    assert dec_seq == 1
    n_pages = page_table.shape[1]
    past = n_pages * PAGE_SIZE
    assert past % SEL_BLOCK == 0 and past % CMP_STRIDE == 0 and past // SEL_BLOCK + 1 >= N_SELECT
    wlen = state_win_kv.shape[2]
    qw = N_HEADS * HEAD_DIM

    a_win = a_w_in[0].astype(BF16)
    a_wout = a_w_out[0].astype(BF16)
    xp1 = _layer_a(x_prompt.reshape(batch * seq, d), norm_g[0], a_win, a_ln_g[0], a_ln_b[0], a_w_s[0], a_b_s[0],
                   a_wout, sample=False)
    xs1, chunk_v = _layer_a(x_sample.reshape(nb, d), norm_g[0], a_win, a_ln_g[0], a_ln_b[0], a_w_s[0], a_b_s[0],
                            a_wout, sample=True)

    w_in = b_w_in[0]
    o1 = qw
    o2 = o1 + N_BRANCH * KVROW
    o3 = o2 + N_BRANCH * qw
    wq_pad = _pad_heads(w_in[:, :o1], N_HEADS).astype(BF16)
    wkv = w_in[:, o1:o2].astype(BF16)
    wq = w_in[:, :o1].astype(BF16)
    wk_sel = w_in[:, o1 + KVROW:o1 + KVROW + KVW]
    wk_win = w_in[:, o1 + 2 * KVROW:o1 + 2 * KVROW + KVW]
    wk_pad = jnp.concatenate([_pad_heads(w, N_KV) for w in (wk_sel, wk_win)], axis=1).astype(BF16)
    wz = w_in[:, o2:o3].astype(BF16)
    n_gate = N_BRANCH * N_HEADS
    n_gate_pad = -(-n_gate // 16) * 16
    wgl = jnp.pad(w_in[:, o3:], ((0, 0), (0, LANE - n_gate))).astype(BF16)
    gb = jnp.pad(b_gate_b[0].reshape(1, n_gate), ((0, 0), (0, LANE - n_gate)))
    wz_t = jnp.transpose(w_in[:, o2:o3]).astype(BF16)
    wgl_t = jnp.pad(jnp.transpose(w_in[:, o3:]), ((0, n_gate_pad - n_gate), (0, 0))).astype(BF16)
    gb_b = jnp.broadcast_to(jnp.pad(b_gate_b[0].reshape(n_gate), (0, n_gate_pad - n_gate))[:, None],
                            (n_gate_pad, LANE))
    wout_t = jnp.transpose(b_w_out[0]).astype(BF16)
    e_np = np.zeros((LANE, n_gate * HEAD_DIM), np.float32)
    for j in range(n_gate):
        e_np[j, j * HEAD_DIM:(j + 1) * HEAD_DIM] = 1.0
    e_mat = jnp.asarray(e_np, dtype=BF16)
    wout = b_w_out[0].astype(BF16)
    w1 = b_cmp_w1[0]
    eye = jnp.eye(N_KV, dtype=F32)
    bd = jnp.einsum("Gg,kjch->kjGcgh", eye, w1).reshape(2, CMP_BLOCK, KVW, KVW)
    wc = jnp.concatenate([bd[:, :CMP_STRIDE], bd[:, CMP_STRIDE:]], axis=-1)
    wc = jnp.transpose(wc, (1, 0, 2, 3)).reshape(2 * CMP_STRIDE, KVW, 2 * KVW).astype(BF16)
    pe = b_cmp_pe[0].reshape(2, 1, CMP_BLOCK * HEAD_DIM)
    w1f = jnp.tile(w1.reshape(2, CMP_BLOCK * HEAD_DIM, HEAD_DIM), (1, 1, N_KV)).astype(BF16)
    b1 = jnp.tile(b_cmp_b1[0], (1, N_KV)).reshape(2, 1, KVW)
    w2bd = jnp.einsum("Gg,kch->kGcgh", eye, b_cmp_w2[0])
    w2p = jnp.pad(w2bd, ((0, 0),) * 4 + ((0, SLOT - HEAD_DIM),)).reshape(2, KVW, N_KV * SLOT).astype(BF16)
    b2p = jnp.pad(jnp.broadcast_to(b_cmp_b2[0][:, None, :], (2, N_KV, HEAD_DIM)),
                  ((0, 0), (0, 0), (0, SLOT - HEAD_DIM))).reshape(2, 1, N_KV * SLOT)

    tabs_q, tabs_k = _rope_tables(jnp.arange(seq, dtype=jnp.int32))
    (qc_t, qr_t, kvc_t, kvs_t, kvw_t, ksa, kwp, vst, vwt) = _project(xp1, norm_g[1], wq, wkv, wk_pad, tabs_q, tabs_k,
                                                                    prompt=True, seq=seq)
    fm = lambda t: t.reshape(t.shape[0], 2, N_KV, HEAD_DIM, t.shape[-1])
    lh_p = _cmp_lh_prompt(fm(kvc_t), wc)
    ov_t = np.zeros((SEL_BLOCK, seq // CMP_STRIDE), np.float32)
    ov_t[:seq // SEL_BLOCK] = _overlap(seq // CMP_STRIDE, seq // SEL_BLOCK).T
    y_prompt = _attn_prompt_t(xp1, qc_t, qr_t, lh_p, ksa, vst, kwp, vwt, norm_g[1], final_norm_g, wz_t, wgl_t, gb_b,
                              wout_t, pe, w1f, b1, w2p, b2p, jnp.asarray(ov_t, dtype=BF16), batch=batch, seq=seq)

    pos_s = jnp.full((nb,), past, dtype=jnp.int32)
    tabs_qs, tabs_ks = _rope_tables(pos_s)
    qc_s, qr_s, kvc_s, kvs_s, kvw_s = _project(xs1, norm_g[1], wq_pad, wkv, None, tabs_qs, tabs_ks,
                                               prompt=False, seq=1)
    page_flat = page_table.reshape(-1).astype(jnp.int32)
    n_phys = cache_cmp_kv.shape[1]
    to_fm = lambda t: jnp.transpose(t, (0, 2, 3, 4, 1))
    lh_s = _cmp_lh_sample(to_fm(cache_cmp_kv[0]), page_flat, wc)
    nc_s = past // CMP_STRIDE
    nb_blk = past // SEL_BLOCK + 1
    nbp = -(-nb_blk // LANE) * LANE
    ov_np = np.zeros((nc_s, nbp), np.float32)
    ov_np[:, :nb_blk] = _overlap(nc_s, nb_blk)
    topi, oc_s = _sample_cmp(lh_s.reshape(nb, nc_s, lh_s.shape[1]), qc_s.reshape(nb, N_HEADS, SLOT),
                             pe, w1f, b1, w2p, b2p, jnp.asarray(ov_np, dtype=BF16), past=past)
    topi_flat = topi.reshape(-1)
    r_gate = xs1.shape[0]
    sz_s, gg_s = pl.pallas_call(
        _gate_kernel, grid=(1,),
        in_specs=[_const_spec(s, single=False) for s in
                  ((r_gate, d), (1, d), wz.shape, wgl.shape, gb.shape, e_mat.shape)],
        out_specs=(_const_spec((r_gate, N_BRANCH * qw), single=False),) * 2,
        out_shape=(jax.ShapeDtypeStruct((r_gate, N_BRANCH * qw), F32),) * 2,
        compiler_params=_params(("arbitrary",)), name="nsa_gate_sample",
    )(xs1, norm_g[1][None, :], wz, wgl, gb, e_mat)
    expand = lambda t: jnp.repeat(t.reshape(nb, N_KV, HEAD_DIM), Q_PER_KV, axis=1)
    y_heads, win_s_t = _sample_attn(
        page_flat, topi_flat, to_fm(cache_sel_kv[0]), qr_s.reshape(nb, N_HEADS, SLOT),
        expand(kvs_s[:, :KVW]), expand(kvs_s[:, KVW:]), expand(kvw_s[:, :KVW]), expand(kvw_s[:, KVW:]),
        to_fm(state_win_kv[0]), kvw_s.reshape(nb, 2, N_KV, HEAD_DIM, 1), oc_s,
        sz_s.reshape(nb, N_BRANCH, N_HEADS, HEAD_DIM), gg_s.reshape(nb, N_BRANCH, N_HEADS, HEAD_DIM),
        n_pages=n_pages)
    y_sample = pl.pallas_call(
        _out_kernel, grid=(1,),
        in_specs=[_const_spec(s, single=False) for s in ((nb, qw), (nb, d), wout.shape, (1, d))],
        out_specs=_const_spec((nb, d), single=False),
        out_shape=jax.ShapeDtypeStruct((nb, d), F32),
        compiler_params=_params(("arbitrary",)), name="nsa_out_sample",
    )(y_heads.reshape(nb, qw), xs1, wout, final_norm_g[None, :])

    kv6 = lambda t: t.reshape(1, nb, 1, 2, N_KV, HEAD_DIM)
    from_fm = lambda t: jnp.transpose(t, (0, 4, 1, 2, 3))[None]
    wl_p = min(WINDOW, seq)
    return (y_prompt.reshape(batch, seq, d), y_sample.reshape(nb, 1, d),
            from_fm(fm(kvc_t)), kv6(kvc_s), from_fm(fm(kvs_t)), kv6(kvs_s),
            from_fm(fm(kvw_t)[..., seq - wl_p:]), from_fm(win_s_t), chunk_v.reshape(1, nb, 1, -1))
```

```python
import functools
import math

import numpy as np
import jax
import jax.numpy as jnp
from jax import lax
from jax.experimental import pallas as pl
from jax.experimental.pallas import tpu as pltpu

F32 = jnp.float32
BF16 = jnp.bfloat16
EPS = 1e-6

A_CHUNK = 128
A_GROUPS = 16
N_HEADS = 16
HEAD_DIM = 64
N_KV = 4
Q_PER_KV = N_HEADS // N_KV
ROT_DIM = HEAD_DIM // 4
ROPE_THETA = 500000.0
CMP_BLOCK = 32
CMP_STRIDE = 16
SEL_BLOCK = 64
N_SELECT = 16
WINDOW = 512
N_BRANCH = 3
PAGE_SIZE = 128

LANE = 128
SLOT = LANE
KVW = N_KV * HEAD_DIM
KVROW = 2 * KVW
TM_ROWS = 256
TQ = 256
PAGES_PER_STEP = 32
VMEM_LIMIT = 56 * 1024 * 1024
MASK_BIG = float(2.0 ** 127)
SQRT_HALF = float(np.sqrt(0.5))
LOG2E = float(1.0 / np.log(2.0))
NEG_INF = float("-inf")


def _rms(x, g):
    return x * lax.rsqrt(jnp.mean(x * x, axis=-1, keepdims=True) + EPS) * g


def _gelu(x):
    return 0.5 * x * (1.0 + lax.erf(x * SQRT_HALF))


def _silu(x):
    return x * jax.nn.sigmoid(x)


def _dot(a, b):
    return jnp.dot(a, b, preferred_element_type=F32)


def _dot_nt(a, b):
    return lax.dot_general(a, b, (((1,), (1,)), ((), ())), preferred_element_type=F32)


def _split3(x):
    a = x.astype(BF16)
    r = x - a.astype(F32)
    b = r.astype(BF16)
    c = (r - b.astype(F32)).astype(BF16)
    return a, b, c


def _rope(a, c_tab, s1_tab, s2_tab):
    w = a.shape[-1]
    reps = w // LANE
    ct = jnp.concatenate([c_tab] * reps, axis=1) if reps > 1 else c_tab
    s1 = jnp.concatenate([s1_tab] * reps, axis=1) if reps > 1 else s1_tab
    s2 = jnp.concatenate([s2_tab] * reps, axis=1) if reps > 1 else s2_tab
    half = ROT_DIM // 2
    return a * ct + pltpu.roll(a, w - half, axis=1) * s1 + pltpu.roll(a, half, axis=1) * s2


def _const_spec(shape, single=True):
    nd = len(shape)
    kw = {"pipeline_mode": pl.Buffered(1)} if single else {}
    return pl.BlockSpec(tuple(shape), lambda *a, _nd=nd: (0,) * _nd, **kw)


def _params(sem):
    return pltpu.CompilerParams(dimension_semantics=sem, vmem_limit_bytes=VMEM_LIMIT)


def _layer_a_kernel(x_ref, g_ref, win_ref, lng_ref, lnb_ref, ws_ref, bs_ref, wout_ref, *rest, sample, tm, aw):
    if sample:
        xo_ref, v_ref = rest
    else:
        xo_ref, mix_ref = rest
    x = x_ref[...]
    xn = _rms(x, g_ref[...]).astype(BF16)
    v = _gelu(_dot(xn, win_ref[:, aw:2 * aw]))
    vc = v - jnp.mean(v, axis=-1, keepdims=True)
    v = vc * lax.rsqrt(jnp.mean(vc * vc, axis=-1, keepdims=True) + EPS) * lng_ref[...] + lnb_ref[...]
    if sample:
        v_ref[...] = v
        mix = v * ws_ref[...] + bs_ref[...]
    else:
        vb = v.astype(BF16)
        tri = (lax.broadcasted_iota(jnp.int32, (A_CHUNK, A_CHUNK), 0)
               >= lax.broadcasted_iota(jnp.int32, (A_CHUNK, A_CHUNK), 1))
        for g in range(A_GROUPS):
            wm = jnp.where(tri, ws_ref[g], 0.0).astype(BF16)
            bias = bs_ref[:, g:g + 1]
            for c in range(tm // A_CHUNK):
                blk = vb[c * A_CHUNK:(c + 1) * A_CHUNK, g * A_CHUNK:(g + 1) * A_CHUNK]
                mix_ref[c * A_CHUNK:(c + 1) * A_CHUNK, g * A_CHUNK:(g + 1) * A_CHUNK] = _dot(wm, blk) + bias
        mix = mix_ref[...]
    u = _gelu(_dot(xn, win_ref[:, 0:aw]))
    z = _dot(xn, win_ref[:, 2 * aw:3 * aw])
    y = (u * mix * _silu(z)).astype(BF16)
    xo_ref[...] = _dot(y, wout_ref[...]) + x


def _layer_a(x2d, norm_g, w_in, ln_g, ln_b, w_s, b_s, w_out, *, sample):
    r, d = x2d.shape
    aw = w_out.shape[0]
    tm = r if sample else TM_ROWS
    assert r % tm == 0 and tm % A_CHUNK == 0 or sample
    if sample:
        ws_arg = jnp.repeat(w_s[:, 0, 0], A_CHUNK)[None, :]
        bs_arg = jnp.repeat(b_s[:, 0], A_CHUNK)[None, :]
    else:
        ws_arg = w_s
        bs_arg = b_s.T
    row = pl.BlockSpec((tm, d), lambda i: (i, 0))
    in_specs = [row, _const_spec((1, d)), _const_spec(w_in.shape), _const_spec((1, aw)), _const_spec((1, aw)),
                _const_spec(ws_arg.shape), _const_spec(bs_arg.shape), _const_spec(w_out.shape)]
    if sample:
        out_shape = (jax.ShapeDtypeStruct((r, d), F32), jax.ShapeDtypeStruct((r, aw), F32))
        out_specs = (row, pl.BlockSpec((tm, aw), lambda i: (i, 0)))
        scratch = []
    else:
        out_shape = jax.ShapeDtypeStruct((r, d), F32)
        out_specs = row
        scratch = [pltpu.VMEM((tm, aw), F32)]
    return pl.pallas_call(
        functools.partial(_layer_a_kernel, sample=sample, tm=tm, aw=aw),
        grid=(r // tm,), in_specs=in_specs, out_specs=out_specs, out_shape=out_shape,
        scratch_shapes=scratch, compiler_params=_params(("parallel",)),
        name="layer_a_sample" if sample else "layer_a_prompt",
    )(x2d, norm_g[None, :], w_in, ln_g[None, :], ln_b[None, :], ws_arg, bs_arg, w_out)


def _proj_kernel(x_ref, g_ref, wq_ref, wkv_ref, *rest, prompt, tm, seq):
    if prompt:
        (wkp_ref, cq_ref, s1q_ref, s2q_ref, ck_ref, s1k_ref, s2k_ref,
         qc_ref, qr_ref, kvc_ref, kvs_ref, kvw_ref, ksa_ref, kwp_ref, vst_ref, vwt_ref) = rest
    else:
        (cq_ref, s1q_ref, s2q_ref, ck_ref, s1k_ref, s2k_ref,
         qc_ref, qr_ref, kvc_ref, kvs_ref, kvw_ref) = rest
    scale = HEAD_DIM ** -0.5
    xn_f = _rms(x_ref[...], g_ref[...])
    xn = xn_f.astype(BF16)
    cq, s1q, s2q = cq_ref[...], s1q_ref[...], s2q_ref[...]
    if not prompt:
        ck, s1k, s2k = ck_ref[...], s1k_ref[...], s2k_ref[...]
        kv = _dot(xn, wkv_ref[...])
        q = _dot(xn, wq_ref[...])
        qc_ref[...] = (q * scale).astype(BF16)
        qr_ref[...] = (_rope(q, cq, s1q, s2q) * scale).astype(BF16)
        kvc_ref[...] = kv[:, 0:KVROW]
        kvs_ref[:, 0:KVW] = _rope(kv[:, KVROW:KVROW + KVW], ck, s1k, s2k)
        kvs_ref[:, KVW:KVROW] = kv[:, KVROW + KVW:2 * KVROW]
        kvw_ref[:, 0:KVW] = _rope(kv[:, 2 * KVROW:2 * KVROW + KVW], ck, s1k, s2k)
        kvw_ref[:, KVW:KVROW] = kv[:, 2 * KVROW + KVW:3 * KVROW]
        return
    xn_t = jnp.transpose(xn_f).astype(BF16)
    cos_t, sin_t = ck_ref[...], s1k_ref[...]
    half = ROT_DIM // 2

    def rope_t(a):
        parts = []
        for h in range(a.shape[0] // HEAD_DIM):
            r0 = h * HEAD_DIM
            x1, x2 = a[r0:r0 + half], a[r0 + half:r0 + 2 * half]
            parts += [x1 * cos_t - x2 * sin_t, x2 * cos_t + x1 * sin_t, a[r0 + 2 * half:r0 + HEAD_DIM]]
        return jnp.concatenate(parts, axis=0)

    q_t = _dot(wq_ref[...], xn_t)
    qc_ref[...] = (q_t * (scale * LOG2E)).astype(BF16)
    qr_ref[...] = (rope_t(q_t) * (scale * LOG2E)).astype(BF16)
    kv_t = _dot(wkv_ref[...], xn_t)
    kvc_ref[...] = kv_t[0:KVROW]
    kvs_t = jnp.concatenate([rope_t(kv_t[KVROW:KVROW + KVW]), kv_t[KVROW + KVW:2 * KVROW]], axis=0)
    kvw_t = jnp.concatenate([rope_t(kv_t[2 * KVROW:2 * KVROW + KVW]), kv_t[2 * KVROW + KVW:3 * KVROW]], axis=0)
    kvs_ref[...] = kvs_t
    kvw_ref[...] = kvw_t
    ones_pad = jnp.where(lax.broadcasted_iota(jnp.int32, (SLOT - HEAD_DIM, tm), 0) == 0, 1.0, 0.0)
    for g in range(N_KV):
        r0 = KVW + g * HEAD_DIM
        vst_ref[g * SLOT:(g + 1) * SLOT, :] = jnp.concatenate([kvs_t[r0:r0 + HEAD_DIM], ones_pad], axis=0).astype(BF16)
        vwt_ref[g * SLOT:(g + 1) * SLOT, :] = jnp.concatenate([kvw_t[r0:r0 + HEAD_DIM], ones_pad], axis=0).astype(BF16)
    pw = N_KV * SLOT
    kp = _dot(xn, wkp_ref[...])
    lane = lax.broadcasted_iota(jnp.int32, (tm, pw), 1) % SLOT
    pos = (pl.program_id(0) % (seq // tm)) * tm + lax.broadcasted_iota(jnp.int32, (tm, pw), 0)
    onehot = jnp.where(pos // SEL_BLOCK == lane - HEAD_DIM, 1.0, 0.0)
    ksa_ref[...] = jnp.where(lane < HEAD_DIM, _rope(kp[:, 0:pw], cq, s1q, s2q), onehot).astype(BF16)
    kwp_ref[...] = _rope(kp[:, pw:2 * pw], cq, s1q, s2q).astype(BF16)


def _project(x2d, norm_g, wq, wkv, wk_pad, tabs_q, tabs_k, *, prompt, seq):
    r, d = x2d.shape
    tm = TQ if prompt else r
    nt = seq // tm if prompt else 1
    row = lambda w: pl.BlockSpec((tm, w), lambda i: (i, 0))
    tab = pl.BlockSpec((tm, LANE), (lambda i: (i % nt, 0)) if prompt else (lambda i: (0, 0)))
    pw = N_KV * SLOT
    in_specs = [row(d), _const_spec((1, d)), _const_spec(wq.shape), _const_spec(wkv.shape)]
    args = [x2d, norm_g[None, :], wq, wkv]
    if prompt:
        in_specs.append(_const_spec(wk_pad.shape))
        args.append(wk_pad)
    tab_k = pl.BlockSpec((ROT_DIM // 2, tm), lambda i: (0, i % nt)) if prompt else tab
    in_specs += [tab] * 3 + [tab_k] * 3
    args += list(tabs_q) + list(tabs_k)
    if prompt:
        nbatch = r // seq
        qd = N_HEADS * HEAD_DIM
        tile_t = lambda rows: pl.BlockSpec((None, None, rows, tm), lambda i: (i // nt, i % nt, 0, 0))
        out_shape = [jax.ShapeDtypeStruct((nbatch, nt, qd, tm), BF16)] * 2
        out_specs = [tile_t(qd)] * 2
        out_shape += [jax.ShapeDtypeStruct((nbatch, KVROW, seq), F32)] * 3
        out_specs += [pl.BlockSpec((None, KVROW, tm), lambda i: (i // nt, 0, i % nt))] * 3
        out_shape += [jax.ShapeDtypeStruct((r, pw), BF16)] * 2
        out_specs += [row(pw)] * 2
        out_shape += [jax.ShapeDtypeStruct((nbatch, nt, pw, tm), BF16)] * 2
        out_specs += [tile_t(pw)] * 2
    else:
        qw = N_HEADS * SLOT
        out_shape = [jax.ShapeDtypeStruct((r, qw), BF16)] * 2
        out_specs = [row(qw), row(qw)]
        out_shape += [jax.ShapeDtypeStruct((r, KVROW), F32)] * 3
        out_specs += [row(KVROW)] * 3
    return pl.pallas_call(
        functools.partial(_proj_kernel, prompt=prompt, tm=tm, seq=seq),
        grid=(r // tm,), in_specs=in_specs, out_specs=tuple(out_specs), out_shape=tuple(out_shape),
        compiler_params=_params(("parallel",)),
        name="nsa_project_prompt" if prompt else "nsa_project_sample",
    )(*args)


def _cmp_lh_kernel(*refs, n_in, n_scalar):
    refs = refs[n_scalar:]
    x_refs = refs[:n_in]
    w_ref, lh_ref, xs_ref = refs[n_in], refs[n_in + 1], refs[n_in + 2]
    half = w_ref.shape[-1]
    tpb = x_refs[0].shape[-1]
    n_rows = lh_ref.shape[0]
    rows_u = n_rows // 2
    tiles_u = rows_u * CMP_STRIDE // LANE
    units = [(kv, hf) for kv in range(2) for hf in range(2)]

    pitch = xs_ref.shape[2] // CMP_STRIDE
    sub = 8

    def fill(u):
        kv, hf = units[u]
        for tt in range(tiles_u):
            tok0 = (hf * tiles_u + tt) * LANE
            r, t0 = x_refs[tok0 // tpb], tok0 % tpb
            for gp in range(N_KV // 2):
                tile = jnp.concatenate([r[kv, 2 * gp, :, t0:t0 + LANE], r[kv, 2 * gp + 1, :, t0:t0 + LANE]], axis=0)
                tile_t = jnp.transpose(tile)
                for v in range(LANE // sub):
                    j0 = (v * sub) % CMP_STRIDE
                    m = (tt * LANE + v * sub) // CMP_STRIDE
                    xs_ref[u, gp, pl.ds(j0 * pitch + m, sub, stride=pitch), :] = tile_t[v * sub:(v + 1) * sub, :]

    def matmuls(u):
        kv, hf = units[u]
        acc = None
        for j in range(CMP_STRIDE):
            xj = jnp.concatenate([xs_ref[u, gp, j * pitch:j * pitch + rows_u, :]
                                  for gp in range(N_KV // 2)], axis=1).astype(BF16)
            t = _dot(xj, w_ref[2 * j + kv])
            acc = t if acc is None else acc + t
        lh_ref[hf * rows_u:(hf + 1) * rows_u, kv * half:(kv + 1) * half] = acc

    fill(0)
    for u in range(len(units)):
        if u + 1 < len(units):
            fill(u + 1)
        matmuls(u)


def _cmp_lh_prompt(kvc_t, wc):
    nbatch, _, _, _, seq = kvc_t.shape
    tt = min(seq, 2048)
    assert seq % tt == 0 and tt % LANE == 0
    nt = seq // tt
    out_w = 2 * wc.shape[-1]
    return pl.pallas_call(
        functools.partial(_cmp_lh_kernel, n_in=1, n_scalar=0),
        grid=(nbatch, nt),
        in_specs=[pl.BlockSpec((None, 2, N_KV, HEAD_DIM, tt), lambda b, i: (b, 0, 0, 0, i)), _const_spec(wc.shape)],
        out_specs=pl.BlockSpec((tt // CMP_STRIDE, out_w), lambda b, i: (b * nt + i, 0)),
        out_shape=jax.ShapeDtypeStruct((nbatch * seq // CMP_STRIDE, out_w), F32),
        scratch_shapes=[pltpu.VMEM((4, N_KV // 2, CMP_STRIDE * (tt // (2 * CMP_STRIDE) + 4), LANE), F32)],
        compiler_params=_params(("parallel", "parallel")), name="cmp_lh_prompt",
    )(kvc_t, wc)


def _cmp_lh_sample(pool_t, page_flat, wc):
    n_pages = page_flat.shape[0]
    pps = PAGES_PER_STEP
    assert n_pages % pps == 0
    steps = n_pages // pps
    blk = (None,) + tuple(pool_t.shape[1:])
    page_specs = [pl.BlockSpec(blk, (lambda i, pt, _p=p: (pt[i * pps + _p], 0, 0, 0, 0))) for p in range(pps)]
    nd = wc.ndim
    w_spec = pl.BlockSpec(wc.shape, lambda i, pt: (0,) * nd, pipeline_mode=pl.Buffered(1))
    out_w = 2 * wc.shape[-1]
    rows = pps * PAGE_SIZE // CMP_STRIDE
    grid_spec = pltpu.PrefetchScalarGridSpec(
        num_scalar_prefetch=1, grid=(steps,),
        in_specs=page_specs + [w_spec],
        out_specs=pl.BlockSpec((rows, out_w), lambda i, pt: (i, 0)),
        scratch_shapes=[pltpu.VMEM((4, N_KV // 2, CMP_STRIDE * (rows // 2 + 4), LANE), F32)])
    return pl.pallas_call(
        functools.partial(_cmp_lh_kernel, n_in=pps, n_scalar=1),
        grid_spec=grid_spec,
        out_shape=jax.ShapeDtypeStruct((steps * rows, out_w), F32),
        compiler_params=_params(("parallel",)), name="cmp_lh_sample",
    )(page_flat, *([pool_t] * pps), wc)


def _compress_finish(lh, pe_ref, w1f_ref, b1_ref, w2_ref, b2_ref, f32_v=False):
    nc = lh.shape[0]
    outs = []
    for kv in range(2):
        pe8 = jnp.broadcast_to(pe_ref[kv], (8, pe_ref.shape[-1])).astype(BF16)
        cb = _dot(pe8, w1f_ref[kv])[0:1, :] + b1_ref[kv]
        lo = lh[:, kv * 2 * KVW:kv * 2 * KVW + KVW]
        hi = lh[:, kv * 2 * KVW + KVW:(kv + 1) * 2 * KVW]
        hi_next = pltpu.roll(hi, nc - 1, axis=0)
        hid = _silu(lo + hi_next + cb)
        out = _dot(hid.astype(BF16), w2_ref[kv]) + b2_ref[kv]
        outs.append(out if (f32_v and kv == 1) else out.astype(BF16))
    return outs


def _select_blocks(imp, blk, jq, axis):
    forced = (blk == 0) | (blk == jq) | (blk == jq - 1)
    score = jnp.where((blk > jq) | (blk < 0), NEG_INF, jnp.where(forced, jnp.inf, imp))
    blk_f = blk.astype(F32)
    sel = jnp.zeros(imp.shape, F32)
    idxs = []
    for _ in range(N_SELECT):
        m = jnp.max(score, axis=axis, keepdims=True)
        idx = jnp.min(jnp.where(score == m, blk_f, 1e9), axis=axis, keepdims=True)
        hit = (blk_f == idx) & (m > NEG_INF)
        sel = jnp.where(hit, 1.0, sel)
        score = jnp.where(blk_f == idx, NEG_INF, score)
        idxs.append(idx)
    return sel, idxs


def _gate(xn, wz_ref, wgl_ref, gb_ref, e_ref):
    z = _dot(xn, wz_ref[...])
    gl = _dot(xn, wgl_ref[...]) + gb_ref[...]
    e = e_ref[...]
    a, b, c = _split3(gl)
    gle = _dot(a, e) + _dot(b, e) + _dot(c, e)
    return _silu(z), jax.nn.sigmoid(gle)


def _gate_kernel(x_ref, g_ref, wz_ref, wgl_ref, gb_ref, e_ref, sz_ref, gg_ref):
    xn = _rms(x_ref[...], g_ref[...]).astype(BF16)
    sz, gg = _gate(xn, wz_ref, wgl_ref, gb_ref, e_ref)
    sz_ref[...] = sz
    gg_ref[...] = gg


def _out_kernel(y_ref, x_ref, w_ref, g_ref, o_ref):
    o_ref[...] = _rms(_dot(y_ref[...].astype(BF16), w_ref[...]) + x_ref[...], g_ref[...])


def _attn_prompt_kernel(x_ref, qc_ref, qr_ref, lh_ref, ksa_ref, vsp_ref,
                        kw0_ref, kw1_ref, kw2_ref, vw0_ref, vw1_ref, vw2_ref,
                        g1_ref, gf_ref, wz_ref, wgl_ref, gb_ref, e_ref, wout_ref,
                        pe_ref, w1f_ref, b1_ref, w2_ref, b2_ref, ovt_ref,
                        out_ref, kc_scr, vc_scr, o_scr, *, tq, seq):
    qi = pl.program_id(1)
    t0 = qi * tq
    nc = seq // CMP_STRIDE
    rows = Q_PER_KV * tq

    @pl.when(qi == 0)
    def _():
        kc, vc = _compress_finish(lh_ref[...], pe_ref, w1f_ref, b1_ref, w2_ref, b2_ref)
        kc_scr[...] = kc
        vc_scr[...] = vc

    a_row = lax.broadcasted_iota(jnp.int32, (rows, 1), 0) % tq
    pos_row = t0 + a_row
    c_col = lax.broadcasted_iota(jnp.int32, (1, tq), 1)
    n_col = lax.broadcasted_iota(jnp.int32, (1, nc), 1)
    mask_c = (n_col * CMP_STRIDE + CMP_BLOCK - 1) <= pos_row
    causal = c_col <= a_row
    blk_t = lax.broadcasted_iota(jnp.int32, (2 * SEL_BLOCK, tq), 0) - SEL_BLOCK
    jq_t = (t0 + lax.broadcasted_iota(jnp.int32, (2 * SEL_BLOCK, tq), 1)) // SEL_BLOCK
    lane_q = lax.broadcasted_iota(jnp.int32, (tq, SLOT), 1)
    tiny = jnp.finfo(jnp.float32).tiny

    def stack(ref, g):
        return jnp.concatenate([ref[:, (Q_PER_KV * g + r) * SLOT:(Q_PER_KV * g + r + 1) * SLOT]
                                for r in range(Q_PER_KV)], axis=0)

    def put(br, g, o):
        for j in range(Q_PER_KV // 2):
            pair = jnp.concatenate([o[(2 * j) * tq:(2 * j + 1) * tq, 0:HEAD_DIM],
                                    o[(2 * j + 1) * tq:(2 * j + 2) * tq, 0:HEAD_DIM]], axis=1)
            h0 = Q_PER_KV * g + 2 * j
            o_scr[br, :, h0 * HEAD_DIM:(h0 + 2) * HEAD_DIM] = pair

    for g in range(N_KV):
        gs = slice(g * SLOT, (g + 1) * SLOT)
        s = _dot_nt(stack(qc_ref, g), kc_scr[:, gs])
        s = jnp.where(mask_c, s, NEG_INF)
        m = jnp.max(s, axis=-1, keepdims=True)
        m = jnp.where(m > NEG_INF, m, 0.0)
        e = jnp.where(mask_c, jnp.exp(s - m), 0.0)
        p = e / jnp.maximum(jnp.sum(e, axis=-1, keepdims=True), tiny)
        put(0, g, _dot(p.astype(BF16), vc_scr[:, gs]))
        psum = p[0:tq] + p[tq:2 * tq] + p[2 * tq:3 * tq] + p[3 * tq:4 * tq]
        ovt = ovt_ref[...]
        pa, pb, pc = _split3(psum)
        imp_t = _dot_nt(ovt, pa) + _dot_nt(ovt, pb) + _dot_nt(ovt, pc)
        sel_t, _ = _select_blocks(imp_t, blk_t, jq_t, axis=0)
        sel = jnp.transpose(sel_t)
        maskpart = jnp.where(lane_q >= HEAD_DIM, (sel - 1.0) * MASK_BIG, 0.0).astype(BF16)
        qr_g = stack(qr_ref, g)
        qa = jnp.where(jnp.concatenate([lane_q] * Q_PER_KV, axis=0) < HEAD_DIM, qr_g,
                       jnp.concatenate([maskpart] * Q_PER_KV, axis=0))

        def sel_tile(kt, carry, diag):
            m_i, l_i, acc = carry
            start = pl.multiple_of(kt * tq, tq)
            k_t = ksa_ref[kt, g * SLOT:(g + 1) * SLOT, :]
            v = vsp_ref[pl.ds(start, tq), gs]
            st = _dot(qa, k_t)
            if diag:
                st = jnp.where(causal, st, -MASK_BIG)
            m_new = jnp.maximum(m_i, jnp.max(st, axis=-1, keepdims=True))
            alpha = jnp.exp(m_i - m_new)
            pt = jnp.exp(st - m_new)
            l_new = alpha * l_i + jnp.sum(pt, axis=-1, keepdims=True)
            acc_new = alpha * acc + _dot(pt.astype(BF16), v)
            return m_new, l_new, acc_new

        init = (jnp.full((rows, 1), NEG_INF, F32), jnp.zeros((rows, 1), F32), jnp.zeros((rows, SLOT), F32))
        carry = lax.fori_loop(0, qi, lambda kt, c: sel_tile(kt, c, False), init)
        _, l_s, acc_s = sel_tile(qi, carry, True)
        put(1, g, acc_s / l_s)
        s0 = _dot(qr_g, kw0_ref[gs, :])
        s1 = _dot(qr_g, kw1_ref[gs, :])
        s2 = _dot(qr_g, kw2_ref[gs, :])
        s0 = jnp.where((c_col > a_row) & (qi >= 2), s0, NEG_INF)
        s1 = jnp.where(qi >= 1, s1, NEG_INF)
        s2 = jnp.where(causal, s2, NEG_INF)
        mw = jnp.maximum(jnp.maximum(jnp.max(s0, axis=-1, keepdims=True), jnp.max(s1, axis=-1, keepdims=True)),
                         jnp.max(s2, axis=-1, keepdims=True))
        p0, p1, p2 = jnp.exp(s0 - mw), jnp.exp(s1 - mw), jnp.exp(s2 - mw)
        lw = (jnp.sum(p0, axis=-1, keepdims=True) + jnp.sum(p1, axis=-1, keepdims=True)
              + jnp.sum(p2, axis=-1, keepdims=True))
        ow = (_dot(p0.astype(BF16), vw0_ref[:, gs]) + _dot(p1.astype(BF16), vw1_ref[:, gs])
              + _dot(p2.astype(BF16), vw2_ref[:, gs]))
        put(2, g, ow / lw)

    x = x_ref[...]
    xn = _rms(x, g1_ref[...]).astype(BF16)
    sz, gg = _gate(xn, wz_ref, wgl_ref, gb_ref, e_ref)
    qw = N_HEADS * HEAD_DIM
    y = None
    for br in range(N_BRANCH):
        t = o_scr[br] * sz[:, br * qw:(br + 1) * qw] * gg[:, br * qw:(br + 1) * qw]
        y = t if y is None else y + t
    out_ref[...] = _rms(_dot(y.astype(BF16), wout_ref[...]) + x, gf_ref[...])


def _attn_prompt(x1, qc, qr, lh, ksa, vsp, kwp, vwp, g1, gf, wz, wgl, gb, e_mat, wout,
                 pe, w1f, b1, w2p, b2p, ovt, *, batch, seq):
    tq = TQ
    assert seq % tq == 0 and WINDOW == 2 * tq and seq % SEL_BLOCK == 0 and seq // SEL_BLOCK <= SEL_BLOCK
    nq = seq // tq
    nc = seq // CMP_STRIDE
    d = x1.shape[1]
    qw = N_HEADS * SLOT
    pw = N_KV * SLOT
    tile = lambda w: pl.BlockSpec((tq, w), lambda b, i: (b * nq + i, 0))
    per_b = lambda rows, w: pl.BlockSpec((rows, w), lambda b, i: (b, 0), pipeline_mode=pl.Buffered(1))
    ksa_spec = pl.BlockSpec((None, nq, pw, tq), lambda b, i: (b, 0, 0, 0), pipeline_mode=pl.Buffered(1))
    kwin = lambda off: pl.BlockSpec((None, None, pw, tq), lambda b, i: (b, jnp.maximum(i - off, 0), 0, 0))
    win = lambda off: pl.BlockSpec((tq, pw), lambda b, i: (b * nq + jnp.maximum(i - off, 0), 0))
    consts = [g1[None, :], gf[None, :], wz, wgl, gb, e_mat, wout, pe, w1f, b1, w2p, b2p, ovt]
    in_specs = ([tile(d), tile(qw), tile(qw), per_b(nc, lh.shape[1]), ksa_spec, per_b(seq, pw),
                 kwin(2), kwin(1), kwin(0), win(2), win(1), win(0)]
                + [_const_spec(c.shape) for c in consts])
    return pl.pallas_call(
        functools.partial(_attn_prompt_kernel, tq=tq, seq=seq),
        grid=(batch, nq), in_specs=in_specs, out_specs=tile(d),
        out_shape=jax.ShapeDtypeStruct((batch * seq, d), F32),
        scratch_shapes=[pltpu.VMEM((nc, pw), BF16), pltpu.VMEM((nc, pw), BF16),
                        pltpu.VMEM((N_BRANCH, tq, N_HEADS * HEAD_DIM), F32)],
        compiler_params=_params(("arbitrary", "arbitrary")), name="nsa_attend_prompt",
    )(x1, qc, qr, lh, ksa, vsp, kwp, kwp, kwp, vwp, vwp, vwp, *consts)


def _attn_t_kernel(x_ref, qc_ref, qr_ref, lh_ref, ksa_ref, vst_ref,
                   kw0_ref, kw1_ref, kw2_ref, vw0_ref, vw1_ref, vw2_ref,
                   g1_ref, gf_ref, wz_ref, wgl_ref, gb_ref, wout_ref,
                   pe_ref, w1f_ref, b1_ref, w2_ref, b2_ref, ov_ref,
                   out_ref, kc_scr, vct_scr, o_scr, *, tq, seq):
    qi = pl.program_id(1)
    t0 = qi * tq
    nc = seq // CMP_STRIDE
    cols = Q_PER_KV * tq
    hd = HEAD_DIM

    @pl.when(qi == 0)
    def _():
        kc, vc = _compress_finish(lh_ref[...], pe_ref, w1f_ref, b1_ref, w2_ref, b2_ref, f32_v=True)
        kc_scr[...] = kc
        vct_scr[...] = jnp.transpose(vc).astype(BF16)

    a_col = lax.broadcasted_iota(jnp.int32, (1, cols), 1) % tq
    pos_col = t0 + a_col
    c_row = lax.broadcasted_iota(jnp.int32, (tq, 1), 0)
    n_row = lax.broadcasted_iota(jnp.int32, (nc, 1), 0)
    mask_c = (n_row * CMP_STRIDE + CMP_BLOCK - 1) <= pos_col
    causal = c_row <= a_col
    blk_t = lax.broadcasted_iota(jnp.int32, (SEL_BLOCK, tq), 0)
    jq_t = (t0 + lax.broadcasted_iota(jnp.int32, (SEL_BLOCK, tq), 1)) // SEL_BLOCK
    tiny = jnp.finfo(jnp.float32).tiny
    zpad = jnp.zeros((SLOT - hd, tq), BF16)

    def stack_t(ref, g, lower):
        return jnp.concatenate(
            [jnp.concatenate([ref[(Q_PER_KV * g + r) * hd:(Q_PER_KV * g + r + 1) * hd, :], lower], axis=0)
             for r in range(Q_PER_KV)], axis=1)

    def put(br, g, o_t):
        for r in range(Q_PER_KV):
            h = Q_PER_KV * g + r
            o_scr[br, h * hd:(h + 1) * hd, :] = o_t[:, r * tq:(r + 1) * tq]

    gsl = lambda g: slice(g * SLOT, (g + 1) * SLOT)

    def cmp_scores(g, n):
        return _dot(kc_scr[0:n, gsl(g)], stack_t(qc_ref, g, zpad))

    def cmp_finish(g, s, n):
        s = jnp.where(mask_c[0:n], s, NEG_INF)
        m = jnp.max(s, axis=0, keepdims=True)
        m = jnp.where(m > NEG_INF, m, 0.0)
        e = jnp.exp2(s - m)
        p = e * (1.0 / jnp.maximum(jnp.sum(e, axis=0, keepdims=True), tiny))
        put(0, g, _dot(vct_scr[gsl(g), 0:n], p.astype(BF16))[0:hd])
        psum = p[:, 0:tq] + p[:, tq:2 * tq] + p[:, 2 * tq:3 * tq] + p[:, 3 * tq:4 * tq]
        ov = ov_ref[:, 0:n]
        pa, pb, pc = _split3(psum)
        return _dot(ov, pa) + _dot(ov, pb) + _dot(ov, pc)


    def win_scores(g):
        qr_t = stack_t(qr_ref, g, zpad)
        return (_dot(kw0_ref[:, gsl(g)], qr_t), _dot(kw1_ref[:, gsl(g)], qr_t), _dot(kw2_ref[:, gsl(g)], qr_t))

    def win_finish(g, ss):
        s0 = jnp.where((c_row > a_col) & (qi >= 2), ss[0], NEG_INF)
        s1 = jnp.where(qi >= 1, ss[1], NEG_INF)
        s2 = jnp.where(causal, ss[2], NEG_INF)
        mw = jnp.maximum(jnp.maximum(jnp.max(s0, axis=0, keepdims=True), jnp.max(s1, axis=0, keepdims=True)),
                         jnp.max(s2, axis=0, keepdims=True))
        acc_w = (_dot(vw0_ref[gsl(g), :], jnp.exp2(s0 - mw).astype(BF16))
                 + _dot(vw1_ref[gsl(g), :], jnp.exp2(s1 - mw).astype(BF16))
                 + _dot(vw2_ref[gsl(g), :], jnp.exp2(s2 - mw).astype(BF16)))
        put(2, g, acc_w[0:hd] * (1.0 / acc_w[hd:hd + 1]))

    imps = [None] * N_KV
    s_c = cmp_scores(0, nc)
    for g in range(N_KV):
        s_next = cmp_scores(g + 1, nc) if g + 1 < N_KV else win_scores(0)
        imps[g] = cmp_finish(g, s_c, nc)
        s_c = s_next
    x = x_ref[...]
    xn_t = jnp.transpose(_rms(x, g1_ref[...])).astype(BF16)
    gate = jax.nn.sigmoid(_dot(wgl_ref[...], xn_t) + jnp.concatenate([gb_ref[...]] * (tq // LANE), axis=1))
    qw = N_HEADS * hd
    szs = [_silu(_dot(wz_ref[br * qw:(br + 1) * qw, :], xn_t)) for br in range(N_BRANCH)]
    sel_all, _ = _select_blocks(jnp.concatenate(imps, axis=1), jnp.concatenate([blk_t] * N_KV, axis=1),
                                jnp.concatenate([jq_t] * N_KV, axis=1), axis=0)
    maskbias_all = ((sel_all - 1.0) * MASK_BIG).astype(BF16)
    s_w = s_c
    for g in range(N_KV):
        s_next = win_scores(g + 1) if g + 1 < N_KV else None
        win_finish(g, s_w)
        s_w = s_next

    qa = [stack_t(qr_ref, g, maskbias_all[:, g * tq:(g + 1) * tq]) for g in range(N_KV)]

    def scores(g, kt):
        start = pl.multiple_of(kt * tq, tq)
        return _dot(ksa_ref[pl.ds(start, tq), gsl(g)], qa[g])

    def consume(g, kt, st, m_i, acc):
        m_new = jnp.maximum(m_i, jnp.max(st, axis=0, keepdims=True))
        alpha = jnp.exp2(m_i - m_new)
        pt = jnp.exp2(st - m_new).astype(BF16)
        return m_new, alpha * acc + _dot(vst_ref[kt, gsl(g), :], pt)

    def all_groups(kt, carry, diag):
        out = []
        st = scores(0, kt)
        for g in range(N_KV):
            st_next = scores(g + 1, kt) if g + 1 < N_KV else None
            if diag:
                st = jnp.where(causal, st, -MASK_BIG)
            out.extend(consume(g, kt, st, carry[2 * g], carry[2 * g + 1]))
            st = st_next
        return tuple(out)

    init = (jnp.full((1, cols), NEG_INF, F32), jnp.zeros((SLOT, cols), F32)) * N_KV
    carry = lax.fori_loop(0, qi // 2,
                          lambda j, c: all_groups(2 * j + 1, all_groups(2 * j, c, False), False), init)
    carry = lax.cond(qi % 2 == 1, lambda c: all_groups(qi - 1, c, False), lambda c: c, carry)
    carry = all_groups(qi, carry, True)
    for g in range(N_KV):
        acc_s = carry[2 * g + 1]
        put(1, g, acc_s[0:hd] * (1.0 / acc_s[hd:hd + 1]))

    y_t = None
    for br in range(N_BRANCH):
        sz = szs[br]
        parts = []
        for h in range(N_HEADS):
            j = br * N_HEADS + h
            parts.append(o_scr[br, h * hd:(h + 1) * hd, :] * sz[h * hd:(h + 1) * hd, :] * gate[j:j + 1, :])
        t = jnp.concatenate(parts, axis=0)
        y_t = t if y_t is None else y_t + t
    o = jnp.transpose(_dot(wout_ref[...], y_t.astype(BF16)))
    out_ref[...] = _rms(o + x, gf_ref[...])


def _attn_prompt_t(x1, qc_t, qr_t, lh, ksa, vst, kwp, vwt, g1, gf, wz_t, wgl_t, gb_b, wout_t,
                   pe, w1f, b1, w2p, b2p, ov, *, batch, seq):
    tq = TQ
    assert seq % tq == 0 and WINDOW == 2 * tq and seq % SEL_BLOCK == 0 and seq // SEL_BLOCK <= SEL_BLOCK
    nq = seq // tq
    nc = seq // CMP_STRIDE
    d = x1.shape[1]
    pw = N_KV * SLOT
    qd = N_HEADS * HEAD_DIM
    tile = lambda w: pl.BlockSpec((tq, w), lambda b, i: (b * nq + i, 0))
    tile_t = lambda rows: pl.BlockSpec((None, None, rows, tq), lambda b, i: (b, i, 0, 0))
    per_b = lambda rows, w: pl.BlockSpec((rows, w), lambda b, i: (b, 0), pipeline_mode=pl.Buffered(1))
    vst_spec = pl.BlockSpec((None, nq, pw, tq), lambda b, i: (b, 0, 0, 0), pipeline_mode=pl.Buffered(1))
    kwin = lambda off: pl.BlockSpec((tq, pw), lambda b, i: (b * nq + jnp.maximum(i - off, 0), 0))
    vwin = lambda off: pl.BlockSpec((None, None, pw, tq), lambda b, i: (b, jnp.maximum(i - off, 0), 0, 0))
    consts = [g1[None, :], gf[None, :], wz_t, wgl_t, gb_b, wout_t, pe, w1f, b1, w2p, b2p, ov]
    in_specs = ([tile(d), tile_t(qd), tile_t(qd), per_b(nc, lh.shape[1]), per_b(seq, pw), vst_spec,
                 kwin(2), kwin(1), kwin(0), vwin(2), vwin(1), vwin(0)]
                + [_const_spec(c.shape) for c in consts])
    return pl.pallas_call(
        functools.partial(_attn_t_kernel, tq=tq, seq=seq),
        grid=(batch, nq), in_specs=in_specs, out_specs=tile(d),
        out_shape=jax.ShapeDtypeStruct((batch * seq, d), F32),
        scratch_shapes=[pltpu.VMEM((nc, pw), BF16), pltpu.VMEM((pw, nc), BF16),
                        pltpu.VMEM((N_BRANCH, qd, tq), F32)],
        compiler_params=_params(("arbitrary", "arbitrary")), name="nsa_attend_prompt",
    )(x1, qc_t, qr_t, lh, ksa, vst, kwp, kwp, kwp, vwt, vwt, vwt, *consts)


def _sample_cmp_kernel(lh_ref, qc_ref, pe_ref, w1f_ref, b1_ref, w2_ref, b2_ref, ov_ref,
                       imp_ref, oc_ref, *, past):
    kc, vc = _compress_finish(lh_ref[...], pe_ref, w1f_ref, b1_ref, w2_ref, b2_ref)
    nc = kc.shape[0]
    nbp = ov_ref.shape[1]
    q = qc_ref[...]
    head_row = lax.broadcasted_iota(jnp.int32, (N_HEADS, 1), 0)
    n_col = lax.broadcasted_iota(jnp.int32, (1, nc), 1)
    mask_c = (n_col * CMP_STRIDE + CMP_BLOCK - 1) <= past
    row8 = lax.broadcasted_iota(jnp.int32, (8, 1), 0)
    tiny = jnp.finfo(jnp.float32).tiny
    oc = jnp.zeros((N_HEADS, SLOT), F32)
    psum8 = jnp.zeros((8, nc), F32)
    for g in range(N_KV):
        gs = slice(g * SLOT, (g + 1) * SLOT)
        in_g = (head_row // Q_PER_KV) == g
        s = jnp.where(mask_c, _dot_nt(q, kc[:, gs]), NEG_INF)
        m = jnp.max(s, axis=-1, keepdims=True)
        m = jnp.where(m > NEG_INF, m, 0.0)
        e = jnp.where(mask_c, jnp.exp(s - m), 0.0)
        p = e / jnp.maximum(jnp.sum(e, axis=-1, keepdims=True), tiny)
        oc = jnp.where(in_g, _dot(p.astype(BF16), vc[:, gs]), oc)
        pg = jnp.sum(jnp.where(in_g, p, 0.0), axis=0, keepdims=True)
        psum8 = jnp.where(row8 == g, pg, psum8)
    ov = ov_ref[...]
    pa, pb, pc = _split3(psum8)
    imp_ref[...] = _dot(pa, ov) + _dot(pb, ov) + _dot(pc, ov)
    oc_ref[...] = oc


def _sample_topk_kernel(imp_ref, topi_ref, *, past):
    imp_t = jnp.transpose(imp_ref[...])
    blk = lax.broadcasted_iota(jnp.int32, imp_t.shape, 0)
    blk = jnp.where(blk < past // SEL_BLOCK + 1, blk, -1)
    _, idxs = _select_blocks(imp_t, blk, jnp.int32(past // SEL_BLOCK), axis=0)
    topi_ref[...] = jnp.concatenate(idxs, axis=0).astype(jnp.int32)


def _sample_cmp(lh3, qc3, pe, w1f, b1, w2p, b2p, ov, *, past):
    nb, nc, w = lh3.shape
    nbp = ov.shape[1]
    consts = [pe, w1f, b1, w2p, b2p, ov]
    imp, oc = pl.pallas_call(
        functools.partial(_sample_cmp_kernel, past=past),
        grid=(nb,),
        in_specs=[pl.BlockSpec((None, nc, w), lambda b: (b, 0, 0)),
                  pl.BlockSpec((None, N_HEADS, SLOT), lambda b: (b, 0, 0))]
                 + [_const_spec(c.shape) for c in consts],
        out_specs=(pl.BlockSpec((None, 8, nbp), lambda b: (b, 0, 0)),
                   pl.BlockSpec((None, N_HEADS, SLOT), lambda b: (b, 0, 0))),
        out_shape=(jax.ShapeDtypeStruct((nb, 8, nbp), F32),
                   jax.ShapeDtypeStruct((nb, N_HEADS, SLOT), F32)),
        compiler_params=_params(("parallel",)), name="nsa_sample_cmp",
    )(lh3, qc3, *consts)
    rows = nb * 8
    topi = pl.pallas_call(
        functools.partial(_sample_topk_kernel, past=past), grid=(1,),
        in_specs=[_const_spec((rows, nbp), single=False)],
        out_specs=_const_spec((N_SELECT, rows), single=False),
        out_shape=jax.ShapeDtypeStruct((N_SELECT, rows), jnp.int32),
        compiler_params=_params(("arbitrary",)), name="nsa_sample_topk",
    )(imp.reshape(rows, nbp))
    topi = jnp.transpose(topi.reshape(N_SELECT, nb, 8)[:, :, :N_KV], (1, 2, 0))
    return topi, oc


def _sample_attn_kernel(pt_ref, ti_ref, *refs, n_pool_blk, wlen):
    kv_refs = refs[:N_SELECT]
    (qr_ref, ksn_ref, vsn_ref, kwn_ref, vwn_ref, win_ref, wnew_ref, oc_ref, sz_ref, gg_ref,
     y_ref, wout_ref) = refs[N_SELECT:]
    b = pl.program_id(0)
    g = pl.program_id(1)
    per_page = PAGE_SIZE // SEL_BLOCK
    q = qr_ref[...]
    qf = q[:, 0:HEAD_DIM].astype(F32)
    in_g = (lax.broadcasted_iota(jnp.int32, (N_HEADS, 1), 0) // Q_PER_KV) == g
    zpad = jnp.zeros((SLOT - HEAD_DIM, PAGE_SIZE), BF16)
    tok_half = lax.broadcasted_iota(jnp.int32, (1, PAGE_SIZE), 1) // SEL_BLOCK

    s_new = jnp.sum(qf * ksn_ref[...], axis=-1, keepdims=True)
    scores = []
    m = s_new
    for r in range(N_SELECT):
        t = ti_ref[(b * N_KV + g) * N_SELECT + r]
        ok = (tok_half == t % per_page) & (t < n_pool_blk)
        k_t = jnp.concatenate([kv_refs[r][0].astype(BF16), zpad], axis=0)
        s = jnp.where(ok, _dot(q, k_t), NEG_INF)
        scores.append(s)
        m = jnp.maximum(m, jnp.max(s, axis=-1, keepdims=True))
    p_new = jnp.exp(s_new - m)
    l = p_new
    o = p_new * vsn_ref[...]
    for r in range(N_SELECT):
        p = jnp.exp(scores[r] - m)
        l = l + jnp.sum(p, axis=-1, keepdims=True)
        o = o + _dot_nt(p.astype(BF16), kv_refs[r][1].astype(BF16))
    o_s = o / l

    wkey = lax.broadcasted_iota(jnp.int32, (1, wlen), 1)
    kw_t = jnp.concatenate([win_ref[0].astype(BF16), jnp.zeros((SLOT - HEAD_DIM, wlen), BF16)], axis=0)
    s = jnp.where(wkey > wlen - WINDOW, _dot(q, kw_t), NEG_INF)
    s_new = jnp.sum(qf * kwn_ref[...], axis=-1, keepdims=True)
    m = jnp.maximum(jnp.max(s, axis=-1, keepdims=True), s_new)
    p = jnp.exp(s - m)
    p_new = jnp.exp(s_new - m)
    l = jnp.sum(p, axis=-1, keepdims=True) + p_new
    o_w = (_dot_nt(p.astype(BF16), win_ref[1].astype(BF16)) + p_new * vwn_ref[...]) / l

    o_c = oc_ref[:, 0:HEAD_DIM]
    y = o_c * sz_ref[0] * gg_ref[0] + o_s * sz_ref[1] * gg_ref[1] + o_w * sz_ref[2] * gg_ref[2]

    @pl.when(g == 0)
    def _():
        y_ref[...] = jnp.zeros(y_ref.shape, F32)

    y_ref[...] = jnp.where(in_g, y, y_ref[...])
    for kv in range(2):
        shifted = pltpu.roll(win_ref[kv], wlen - 1, axis=1)
        wout_ref[kv] = jnp.where(wkey == wlen - 1, wnew_ref[kv], shifted)


def _sample_attn(page_flat, topi_flat, pool_t, qr3, ksn, vsn, kwn, vwn, win_t, wnew, oc, sz4, gg4, *, n_pages):
    nb = qr3.shape[0]
    wlen = win_t.shape[-1]
    per_page = PAGE_SIZE // SEL_BLOCK
    n_pool_blk = n_pages * per_page

    def pool_map(r):
        def f(b, g, pt, ti):
            blk = jnp.minimum(ti[(b * N_KV + g) * N_SELECT + r], n_pool_blk - 1)
            return (pt[b * n_pages + blk // per_page], 0, g, 0, 0)
        return f

    b3 = lambda s1, s2: pl.BlockSpec((None, s1, s2), lambda b, g, pt, ti: (b, 0, 0))
    b4 = pl.BlockSpec((None, N_BRANCH, N_HEADS, HEAD_DIM), lambda b, g, pt, ti: (b, 0, 0, 0))
    page = lambda r: pl.BlockSpec((None, 2, None, HEAD_DIM, PAGE_SIZE), pool_map(r))
    wspec = lambda last: pl.BlockSpec((None, 2, None, HEAD_DIM, last), lambda b, g, pt, ti: (b, 0, g, 0, 0))
    in_specs = ([page(r) for r in range(N_SELECT)]
                + [b3(N_HEADS, SLOT)] + [b3(N_HEADS, HEAD_DIM)] * 4
                + [wspec(wlen), wspec(1), b3(N_HEADS, SLOT), b4, b4])
    grid_spec = pltpu.PrefetchScalarGridSpec(
        num_scalar_prefetch=2, grid=(nb, N_KV), in_specs=in_specs,
        out_specs=(b3(N_HEADS, HEAD_DIM), wspec(wlen)))
    return pl.pallas_call(
        functools.partial(_sample_attn_kernel, n_pool_blk=n_pool_blk, wlen=wlen),
        grid_spec=grid_spec,
        out_shape=(jax.ShapeDtypeStruct((nb, N_HEADS, HEAD_DIM), F32),
                   jax.ShapeDtypeStruct(win_t.shape, F32)),
        compiler_params=_params(("arbitrary", "arbitrary")), name="nsa_sample_sel_win",
    )(page_flat, topi_flat, *([pool_t] * N_SELECT), qr3, ksn, vsn, kwn, vwn, win_t, wnew, oc, sz4, gg4)


def _rope_tables(pos):
    half = ROT_DIM // 2
    freqs = jnp.exp(-math.log(ROPE_THETA) * jnp.arange(half, dtype=F32) * (2.0 / ROT_DIM))
    ang = pos.astype(F32)[:, None] * freqs[None, :]
    cos, sin = jnp.cos(ang), jnp.sin(ang)
    n = pos.shape[0]
    one = jnp.ones((n, HEAD_DIM - ROT_DIM), F32)
    zero8 = jnp.zeros((n, half), F32)
    zrest = jnp.zeros((n, HEAD_DIM - ROT_DIM), F32)
    c64 = jnp.concatenate([cos, cos, one], axis=1)
    s1_64 = jnp.concatenate([-sin, zero8, zrest], axis=1)
    s2_64 = jnp.concatenate([zero8, sin, zrest], axis=1)
    pad1 = jnp.ones((n, SLOT - HEAD_DIM), F32)
    pad0 = jnp.zeros((n, SLOT - HEAD_DIM), F32)
    tabs_q = (jnp.concatenate([c64, pad1], axis=1), jnp.concatenate([s1_64, pad0], axis=1),
              jnp.concatenate([s2_64, pad0], axis=1))
    tabs_k = tuple(jnp.concatenate([t, t], axis=1) for t in (c64, s1_64, s2_64))
    tabs_t = (jnp.transpose(cos), jnp.transpose(sin), jnp.transpose(cos))
    return tabs_q, tabs_k, tabs_t


def _pad_heads(w, n_heads):
    k = w.shape[0]
    w3 = w.reshape(k, n_heads, HEAD_DIM)
    return jnp.pad(w3, ((0, 0), (0, 0), (0, SLOT - HEAD_DIM))).reshape(k, n_heads * SLOT)


def _overlap(nc, nb):
    n = np.arange(nc)[:, None] * CMP_STRIDE
    j = np.arange(nb)[None, :] * SEL_BLOCK
    return ((n <= j + SEL_BLOCK - 1) & (n + CMP_BLOCK - 1 >= j)).astype(np.float32)


def kernel(x_prompt, x_sample, cache_cmp_kv, cache_sel_kv, state_win_kv, page_table, norm_g, final_norm_g,
           a_w_in, a_ln_g, a_ln_b, a_w_s, a_b_s, a_w_out, b_w_in, b_cmp_pe, b_cmp_w1, b_cmp_b1, b_cmp_w2,
           b_cmp_b2, b_gate_b, b_w_out):
    batch, seq, d = x_prompt.shape
    nb, dec_seq, _ = x_sample.shape
    assert dec_seq == 1
    n_pages = page_table.shape[1]
    past = n_pages * PAGE_SIZE
    assert past % SEL_BLOCK == 0 and past % CMP_STRIDE == 0 and past // SEL_BLOCK + 1 >= N_SELECT
    wlen = state_win_kv.shape[2]
    qw = N_HEADS * HEAD_DIM

    a_win = a_w_in[0].astype(BF16)
    a_wout = a_w_out[0].astype(BF16)
    xp1 = _layer_a(x_prompt.reshape(batch * seq, d), norm_g[0], a_win, a_ln_g[0], a_ln_b[0], a_w_s[0], a_b_s[0],
                   a_wout, sample=False)
    xs1, chunk_v = _layer_a(x_sample.reshape(nb, d), norm_g[0], a_win, a_ln_g[0], a_ln_b[0], a_w_s[0], a_b_s[0],
                            a_wout, sample=True)

    w_in = b_w_in[0]
    o1 = qw
    o2 = o1 + N_BRANCH * KVROW
    o3 = o2 + N_BRANCH * qw
    wq_pad = _pad_heads(w_in[:, :o1], N_HEADS).astype(BF16)
    wkv = w_in[:, o1:o2].astype(BF16)
    wk_sel = w_in[:, o1 + KVROW:o1 + KVROW + KVW]
    wk_win = w_in[:, o1 + 2 * KVROW:o1 + 2 * KVROW + KVW]
    wk_pad = jnp.concatenate([_pad_heads(w, N_KV) for w in (wk_sel, wk_win)], axis=1).astype(BF16)
    wz = w_in[:, o2:o3].astype(BF16)
    n_gate = N_BRANCH * N_HEADS
    n_gate_pad = -(-n_gate // 16) * 16
    wgl = jnp.pad(w_in[:, o3:], ((0, 0), (0, LANE - n_gate))).astype(BF16)
    gb = jnp.pad(b_gate_b[0].reshape(1, n_gate), ((0, 0), (0, LANE - n_gate)))
    wz_t = jnp.transpose(w_in[:, o2:o3]).astype(BF16)
    wgl_t = jnp.pad(jnp.transpose(w_in[:, o3:]), ((0, n_gate_pad - n_gate), (0, 0))).astype(BF16)
    gb_b = jnp.broadcast_to(jnp.pad(b_gate_b[0].reshape(n_gate), (0, n_gate_pad - n_gate))[:, None],
                            (n_gate_pad, LANE))
    wout_t = jnp.transpose(b_w_out[0]).astype(BF16)
    e_np = np.zeros((LANE, n_gate * HEAD_DIM), np.float32)
    for j in range(n_gate):
        e_np[j, j * HEAD_DIM:(j + 1) * HEAD_DIM] = 1.0
    e_mat = jnp.asarray(e_np, dtype=BF16)
    wout = b_w_out[0].astype(BF16)
    w1 = b_cmp_w1[0]
    eye = jnp.eye(N_KV, dtype=F32)
    bd = jnp.einsum("Gg,kjch->kjGcgh", eye, w1).reshape(2, CMP_BLOCK, KVW, KVW)
    wc = jnp.concatenate([bd[:, :CMP_STRIDE], bd[:, CMP_STRIDE:]], axis=-1)
    wc = jnp.transpose(wc, (1, 0, 2, 3)).reshape(2 * CMP_STRIDE, KVW, 2 * KVW).astype(BF16)
    pe = b_cmp_pe[0].reshape(2, 1, CMP_BLOCK * HEAD_DIM)
    w1f = jnp.tile(w1.reshape(2, CMP_BLOCK * HEAD_DIM, HEAD_DIM), (1, 1, N_KV)).astype(BF16)
    b1 = jnp.tile(b_cmp_b1[0], (1, N_KV)).reshape(2, 1, KVW)
    w2bd = jnp.einsum("Gg,kch->kGcgh", eye, b_cmp_w2[0])
    w2p = jnp.pad(w2bd, ((0, 0),) * 4 + ((0, SLOT - HEAD_DIM),)).reshape(2, KVW, N_KV * SLOT).astype(BF16)
    b2p = jnp.pad(jnp.broadcast_to(b_cmp_b2[0][:, None, :], (2, N_KV, HEAD_DIM)),
                  ((0, 0), (0, 0), (0, SLOT - HEAD_DIM))).reshape(2, 1, N_KV * SLOT)

    tabs_q, _, tabs_t = _rope_tables(jnp.arange(seq, dtype=jnp.int32))
    wq_t = jnp.transpose(w_in[:, :o1]).astype(BF16)
    wkv_t = jnp.transpose(w_in[:, o1:o2]).astype(BF16)
    (qc_t, qr_t, kvc_t, kvs_t, kvw_t, ksa, kwp, vst, vwt) = _project(xp1, norm_g[1], wq_t, wkv_t, wk_pad, tabs_q,
                                                                    tabs_t, prompt=True, seq=seq)
    fm = lambda t: t.reshape(t.shape[0], 2, N_KV, HEAD_DIM, t.shape[-1])
    lh_p = _cmp_lh_prompt(fm(kvc_t), wc)
    ov_t = np.zeros((SEL_BLOCK, seq // CMP_STRIDE), np.float32)
    ov_t[:seq // SEL_BLOCK] = _overlap(seq // CMP_STRIDE, seq // SEL_BLOCK).T
    y_prompt = _attn_prompt_t(xp1, qc_t, qr_t, lh_p, ksa, vst, kwp, vwt, norm_g[1], final_norm_g, wz_t, wgl_t, gb_b,
                              wout_t, pe, w1f, b1, w2p, b2p, jnp.asarray(ov_t, dtype=BF16), batch=batch, seq=seq)

    pos_s = jnp.full((nb,), past, dtype=jnp.int32)
    tabs_qs, tabs_ks, _ = _rope_tables(pos_s)
    qc_s, qr_s, kvc_s, kvs_s, kvw_s = _project(xs1, norm_g[1], wq_pad, wkv, None, tabs_qs, tabs_ks,
                                               prompt=False, seq=1)
    page_flat = page_table.reshape(-1).astype(jnp.int32)
    n_phys = cache_cmp_kv.shape[1]
    to_fm = lambda t: jnp.transpose(t, (0, 2, 3, 4, 1))
    lh_s = _cmp_lh_sample(to_fm(cache_cmp_kv[0]), page_flat, wc)
    nc_s = past // CMP_STRIDE
    nb_blk = past // SEL_BLOCK + 1
    nbp = -(-nb_blk // LANE) * LANE
    ov_np = np.zeros((nc_s, nbp), np.float32)
    ov_np[:, :nb_blk] = _overlap(nc_s, nb_blk)
    topi, oc_s = _sample_cmp(lh_s.reshape(nb, nc_s, lh_s.shape[1]), qc_s.reshape(nb, N_HEADS, SLOT),
                             pe, w1f, b1, w2p, b2p, jnp.asarray(ov_np, dtype=BF16), past=past)
    topi_flat = topi.reshape(-1)
    r_gate = xs1.shape[0]
    sz_s, gg_s = pl.pallas_call(
        _gate_kernel, grid=(1,),
        in_specs=[_const_spec(s, single=False) for s in
                  ((r_gate, d), (1, d), wz.shape, wgl.shape, gb.shape, e_mat.shape)],
        out_specs=(_const_spec((r_gate, N_BRANCH * qw), single=False),) * 2,
        out_shape=(jax.ShapeDtypeStruct((r_gate, N_BRANCH * qw), F32),) * 2,
        compiler_params=_params(("arbitrary",)), name="nsa_gate_sample",
    )(xs1, norm_g[1][None, :], wz, wgl, gb, e_mat)
    expand = lambda t: jnp.repeat(t.reshape(nb, N_KV, HEAD_DIM), Q_PER_KV, axis=1)
    y_heads, win_s_t = _sample_attn(
        page_flat, topi_flat, to_fm(cache_sel_kv[0]), qr_s.reshape(nb, N_HEADS, SLOT),
        expand(kvs_s[:, :KVW]), expand(kvs_s[:, KVW:]), expand(kvw_s[:, :KVW]), expand(kvw_s[:, KVW:]),
        to_fm(state_win_kv[0]), kvw_s.reshape(nb, 2, N_KV, HEAD_DIM, 1), oc_s,
        sz_s.reshape(nb, N_BRANCH, N_HEADS, HEAD_DIM), gg_s.reshape(nb, N_BRANCH, N_HEADS, HEAD_DIM),
        n_pages=n_pages)
    y_sample = pl.pallas_call(
        _out_kernel, grid=(1,),
        in_specs=[_const_spec(s, single=False) for s in ((nb, qw), (nb, d), wout.shape, (1, d))],
        out_specs=_const_spec((nb, d), single=False),
        out_shape=jax.ShapeDtypeStruct((nb, d), F32),
        compiler_params=_params(("arbitrary",)), name="nsa_out_sample",
    )(y_heads.reshape(nb, qw), xs1, wout, final_norm_g[None, :])

    kv6 = lambda t: t.reshape(1, nb, 1, 2, N_KV, HEAD_DIM)
    from_fm = lambda t: jnp.transpose(t, (0, 4, 1, 2, 3))[None]
    wl_p = min(WINDOW, seq)
    return (y_prompt.reshape(batch, seq, d), y_sample.reshape(nb, 1, d),
            from_fm(fm(kvc_t)), kv6(kvc_s), from_fm(fm(kvs_t)), kv6(kvs_s),
            from_fm(fm(kvw_t)[..., seq - wl_p:]), from_fm(win_s_t), chunk_v.reshape(1, nb, 1, -1))
```

```python
import functools
import math

import numpy as np
import jax
import jax.numpy as jnp
from jax import lax
from jax.experimental import pallas as pl
from jax.experimental.pallas import tpu as pltpu

F32 = jnp.float32
BF16 = jnp.bfloat16
EPS = 1e-6

A_CHUNK = 128
A_GROUPS = 16
N_HEADS = 16
HEAD_DIM = 64
N_KV = 4
Q_PER_KV = N_HEADS // N_KV
ROT_DIM = HEAD_DIM // 4
ROPE_THETA = 500000.0
CMP_BLOCK = 32
CMP_STRIDE = 16
SEL_BLOCK = 64
N_SELECT = 16
WINDOW = 512
N_BRANCH = 3
PAGE_SIZE = 128

LANE = 128
SLOT = LANE
KVW = N_KV * HEAD_DIM
KVROW = 2 * KVW
TM_ROWS = 256
TQ = 256
PAGES_PER_STEP = 32
VMEM_LIMIT = 56 * 1024 * 1024
MASK_BIG = float(2.0 ** 127)
SQRT_HALF = float(np.sqrt(0.5))
LOG2E = float(1.0 / np.log(2.0))
NEG_INF = float("-inf")


def _rms(x, g):
    return x * lax.rsqrt(jnp.mean(x * x, axis=-1, keepdims=True) + EPS) * g


def _gelu(x):
    return 0.5 * x * (1.0 + lax.erf(x * SQRT_HALF))


def _silu(x):
    return x * jax.nn.sigmoid(x)


def _dot(a, b):
    return jnp.dot(a, b, preferred_element_type=F32)


def _dot_nt(a, b):
    return lax.dot_general(a, b, (((1,), (1,)), ((), ())), preferred_element_type=F32)


def _split3(x):
    a = x.astype(BF16)
    r = x - a.astype(F32)
    b = r.astype(BF16)
    c = (r - b.astype(F32)).astype(BF16)
    return a, b, c


def _rope(a, c_tab, s1_tab, s2_tab):
    w = a.shape[-1]
    reps = w // LANE
    ct = jnp.concatenate([c_tab] * reps, axis=1) if reps > 1 else c_tab
    s1 = jnp.concatenate([s1_tab] * reps, axis=1) if reps > 1 else s1_tab
    s2 = jnp.concatenate([s2_tab] * reps, axis=1) if reps > 1 else s2_tab
    half = ROT_DIM // 2
    return a * ct + pltpu.roll(a, w - half, axis=1) * s1 + pltpu.roll(a, half, axis=1) * s2


def _const_spec(shape, single=True):
    nd = len(shape)
    kw = {"pipeline_mode": pl.Buffered(1)} if single else {}
    return pl.BlockSpec(tuple(shape), lambda *a, _nd=nd: (0,) * _nd, **kw)


def _params(sem):
    return pltpu.CompilerParams(dimension_semantics=sem, vmem_limit_bytes=VMEM_LIMIT)


def _layer_a_kernel(x_ref, g_ref, win_ref, lng_ref, lnb_ref, ws_ref, bs_ref, wout_ref, *rest, sample, tm, aw):
    if sample:
        xo_ref, v_ref = rest
    else:
        xo_ref, mix_ref = rest
    x = x_ref[...]
    xn = _rms(x, g_ref[...]).astype(BF16)
    v = _gelu(_dot(xn, win_ref[:, aw:2 * aw]))
    vc = v - jnp.mean(v, axis=-1, keepdims=True)
    v = vc * lax.rsqrt(jnp.mean(vc * vc, axis=-1, keepdims=True) + EPS) * lng_ref[...] + lnb_ref[...]
    if sample:
        v_ref[...] = v
        mix = v * ws_ref[...] + bs_ref[...]
    else:
        vb = v.astype(BF16)
        tri = (lax.broadcasted_iota(jnp.int32, (A_CHUNK, A_CHUNK), 0)
               >= lax.broadcasted_iota(jnp.int32, (A_CHUNK, A_CHUNK), 1))
        for g in range(A_GROUPS):
            wm = jnp.where(tri, ws_ref[g], 0.0).astype(BF16)
            bias = bs_ref[:, g:g + 1]
            for c in range(tm // A_CHUNK):
                blk = vb[c * A_CHUNK:(c + 1) * A_CHUNK, g * A_CHUNK:(g + 1) * A_CHUNK]
                mix_ref[c * A_CHUNK:(c + 1) * A_CHUNK, g * A_CHUNK:(g + 1) * A_CHUNK] = _dot(wm, blk) + bias
        mix = mix_ref[...]
    u = _gelu(_dot(xn, win_ref[:, 0:aw]))
    z = _dot(xn, win_ref[:, 2 * aw:3 * aw])
    y = (u * mix * _silu(z)).astype(BF16)
    xo_ref[...] = _dot(y, wout_ref[...]) + x


def _layer_a(x2d, norm_g, w_in, ln_g, ln_b, w_s, b_s, w_out, *, sample):
    r, d = x2d.shape
    aw = w_out.shape[0]
    tm = r if sample else TM_ROWS
    assert r % tm == 0 and tm % A_CHUNK == 0 or sample
    if sample:
        ws_arg = jnp.repeat(w_s[:, 0, 0], A_CHUNK)[None, :]
        bs_arg = jnp.repeat(b_s[:, 0], A_CHUNK)[None, :]
    else:
        ws_arg = w_s
        bs_arg = b_s.T
    row = pl.BlockSpec((tm, d), lambda i: (i, 0))
    in_specs = [row, _const_spec((1, d)), _const_spec(w_in.shape), _const_spec((1, aw)), _const_spec((1, aw)),
                _const_spec(ws_arg.shape), _const_spec(bs_arg.shape), _const_spec(w_out.shape)]
    if sample:
        out_shape = (jax.ShapeDtypeStruct((r, d), F32), jax.ShapeDtypeStruct((r, aw), F32))
        out_specs = (row, pl.BlockSpec((tm, aw), lambda i: (i, 0)))
        scratch = []
    else:
        out_shape = jax.ShapeDtypeStruct((r, d), F32)
        out_specs = row
        scratch = [pltpu.VMEM((tm, aw), F32)]
    return pl.pallas_call(
        functools.partial(_layer_a_kernel, sample=sample, tm=tm, aw=aw),
        grid=(r // tm,), in_specs=in_specs, out_specs=out_specs, out_shape=out_shape,
        scratch_shapes=scratch, compiler_params=_params(("parallel",)),
        name="layer_a_sample" if sample else "layer_a_prompt",
    )(x2d, norm_g[None, :], w_in, ln_g[None, :], ln_b[None, :], ws_arg, bs_arg, w_out)


def _proj_kernel(x_ref, g_ref, wq_ref, wkv_ref, *rest, prompt, tm, seq):
    if prompt:
        (wkp_ref, cq_ref, s1q_ref, s2q_ref, ck_ref, s1k_ref, s2k_ref,
         qc_ref, qr_ref, kvc_ref, kvs_ref, kvw_ref, ksa_ref, kwp_ref, vst_ref, vwt_ref) = rest
    else:
        (cq_ref, s1q_ref, s2q_ref, ck_ref, s1k_ref, s2k_ref,
         qc_ref, qr_ref, kvc_ref, kvs_ref, kvw_ref) = rest
    scale = HEAD_DIM ** -0.5
    xn_f = _rms(x_ref[...], g_ref[...])
    xn = xn_f.astype(BF16)
    cq, s1q, s2q = cq_ref[...], s1q_ref[...], s2q_ref[...]
    if not prompt:
        ck, s1k, s2k = ck_ref[...], s1k_ref[...], s2k_ref[...]
        kv = _dot(xn, wkv_ref[...])
        q = _dot(xn, wq_ref[...])
        qc_ref[...] = (q * scale).astype(BF16)
        qr_ref[...] = (_rope(q, cq, s1q, s2q) * scale).astype(BF16)
        kvc_ref[...] = kv[:, 0:KVROW]
        kvs_ref[:, 0:KVW] = _rope(kv[:, KVROW:KVROW + KVW], ck, s1k, s2k)
        kvs_ref[:, KVW:KVROW] = kv[:, KVROW + KVW:2 * KVROW]
        kvw_ref[:, 0:KVW] = _rope(kv[:, 2 * KVROW:2 * KVROW + KVW], ck, s1k, s2k)
        kvw_ref[:, KVW:KVROW] = kv[:, 2 * KVROW + KVW:3 * KVROW]
        return
    xn_t = jnp.transpose(xn_f).astype(BF16)
    cos_t, sin_t = ck_ref[...], s1k_ref[...]
    half = ROT_DIM // 2

    def rope_t(a):
        parts = []
        for h in range(a.shape[0] // HEAD_DIM):
            r0 = h * HEAD_DIM
            x1, x2 = a[r0:r0 + half], a[r0 + half:r0 + 2 * half]
            parts += [x1 * cos_t - x2 * sin_t, x2 * cos_t + x1 * sin_t, a[r0 + 2 * half:r0 + HEAD_DIM]]
        return jnp.concatenate(parts, axis=0)

    q_t = _dot(wq_ref[...], xn_t)
    qc_ref[...] = (q_t * (scale * LOG2E)).astype(BF16)
    qr_ref[...] = (rope_t(q_t) * (scale * LOG2E)).astype(BF16)
    kv_t = _dot(wkv_ref[...], xn_t)
    kvc_ref[...] = kv_t[0:KVROW]
    kvs_t = jnp.concatenate([rope_t(kv_t[KVROW:KVROW + KVW]), kv_t[KVROW + KVW:2 * KVROW]], axis=0)
    kvw_t = jnp.concatenate([rope_t(kv_t[2 * KVROW:2 * KVROW + KVW]), kv_t[2 * KVROW + KVW:3 * KVROW]], axis=0)
    kvs_ref[...] = kvs_t
    kvw_ref[...] = kvw_t
    ones_pad = jnp.where(lax.broadcasted_iota(jnp.int32, (SLOT - HEAD_DIM, tm), 0) == 0, 1.0, 0.0)
    for g in range(N_KV):
        r0 = KVW + g * HEAD_DIM
        vst_ref[g * SLOT:(g + 1) * SLOT, :] = jnp.concatenate([kvs_t[r0:r0 + HEAD_DIM], ones_pad], axis=0).astype(BF16)
        vwt_ref[g * SLOT:(g + 1) * SLOT, :] = jnp.concatenate([kvw_t[r0:r0 + HEAD_DIM], ones_pad], axis=0).astype(BF16)
    pw = N_KV * SLOT
    kp = _dot(xn, wkp_ref[...])
    lane = lax.broadcasted_iota(jnp.int32, (tm, pw), 1) % SLOT
    pos = (pl.program_id(0) % (seq // tm)) * tm + lax.broadcasted_iota(jnp.int32, (tm, pw), 0)
    onehot = jnp.where(pos // SEL_BLOCK == lane - HEAD_DIM, 1.0, 0.0)
    ksa_ref[...] = jnp.where(lane < HEAD_DIM, _rope(kp[:, 0:pw], cq, s1q, s2q), onehot).astype(BF16)
    kwp_ref[...] = _rope(kp[:, pw:2 * pw], cq, s1q, s2q).astype(BF16)


def _project(x2d, norm_g, wq, wkv, wk_pad, tabs_q, tabs_k, *, prompt, seq):
    r, d = x2d.shape
    tm = TQ if prompt else r
    nt = seq // tm if prompt else 1
    row = lambda w: pl.BlockSpec((tm, w), lambda i: (i, 0))
    tab = pl.BlockSpec((tm, LANE), (lambda i: (i % nt, 0)) if prompt else (lambda i: (0, 0)))
    pw = N_KV * SLOT
    in_specs = [row(d), _const_spec((1, d)), _const_spec(wq.shape), _const_spec(wkv.shape)]
    args = [x2d, norm_g[None, :], wq, wkv]
    if prompt:
        in_specs.append(_const_spec(wk_pad.shape))
        args.append(wk_pad)
    tab_k = pl.BlockSpec((ROT_DIM // 2, tm), lambda i: (0, i % nt)) if prompt else tab
    in_specs += [tab] * 3 + [tab_k] * 3
    args += list(tabs_q) + list(tabs_k)
    if prompt:
        nbatch = r // seq
        qd = N_HEADS * HEAD_DIM
        tile_t = lambda rows: pl.BlockSpec((None, None, rows, tm), lambda i: (i // nt, i % nt, 0, 0))
        out_shape = [jax.ShapeDtypeStruct((nbatch, nt, qd, tm), BF16)] * 2
        out_specs = [tile_t(qd)] * 2
        out_shape += [jax.ShapeDtypeStruct((nbatch, KVROW, seq), F32)] * 3
        out_specs += [pl.BlockSpec((None, KVROW, tm), lambda i: (i // nt, 0, i % nt))] * 3
        out_shape += [jax.ShapeDtypeStruct((r, pw), BF16)] * 2
        out_specs += [row(pw)] * 2
        out_shape += [jax.ShapeDtypeStruct((nbatch, nt, pw, tm), BF16)] * 2
        out_specs += [tile_t(pw)] * 2
    else:
        qw = N_HEADS * SLOT
        out_shape = [jax.ShapeDtypeStruct((r, qw), BF16)] * 2
        out_specs = [row(qw), row(qw)]
        out_shape += [jax.ShapeDtypeStruct((r, KVROW), F32)] * 3
        out_specs += [row(KVROW)] * 3
    return pl.pallas_call(
        functools.partial(_proj_kernel, prompt=prompt, tm=tm, seq=seq),
        grid=(r // tm,), in_specs=in_specs, out_specs=tuple(out_specs), out_shape=tuple(out_shape),
        compiler_params=_params(("parallel",)),
        name="nsa_project_prompt" if prompt else "nsa_project_sample",
    )(*args)


def _cmp_lh_kernel(*refs, n_in, n_scalar):
    refs = refs[n_scalar:]
    x_refs = refs[:n_in]
    w_ref, lh_ref, xs_ref = refs[n_in], refs[n_in + 1], refs[n_in + 2]
    half = 2 * KVW
    tpb = x_refs[0].shape[-1]
    n_rows = lh_ref.shape[0]
    rows_u = n_rows // 2
    tiles_u = rows_u * CMP_STRIDE // LANE
    units = [(kv, hf) for kv in range(2) for hf in range(2)]

    pitch = xs_ref.shape[2] // CMP_STRIDE
    sub = 8
    eye_bf = jnp.where(lax.broadcasted_iota(jnp.int32, (LANE, LANE), 0)
                       == lax.broadcasted_iota(jnp.int32, (LANE, LANE), 1), 1.0, 0.0).astype(BF16)

    def fill(u):
        kv, hf = units[u]
        for tt in range(tiles_u):
            tok0 = (hf * tiles_u + tt) * LANE
            r, t0 = x_refs[tok0 // tpb], tok0 % tpb
            for gp in range(N_KV // 2):
                tile = jnp.concatenate([r[kv, 2 * gp, :, t0:t0 + LANE], r[kv, 2 * gp + 1, :, t0:t0 + LANE]], axis=0)
                tile_t = jnp.transpose(tile) if gp == 0 else _dot_nt(eye_bf, tile.astype(BF16))
                for v in range(LANE // sub):
                    j0 = (v * sub) % CMP_STRIDE
                    m = (tt * LANE + v * sub) // CMP_STRIDE
                    xs_ref[u, gp, pl.ds(j0 * pitch + m, sub, stride=pitch), :] = tile_t[v * sub:(v + 1) * sub, :]

    def matmuls(u):
        kv, hf = units[u]
        rows = slice(hf * rows_u, (hf + 1) * rows_u)
        for gp in range(N_KV // 2):
            x = jnp.concatenate([xs_ref[u, gp, j * pitch:j * pitch + rows_u, :] for j in range(CMP_STRIDE)],
                                axis=1).astype(BF16)
            acc = _dot(x, w_ref[kv])
            lh_ref[rows, kv * half + gp * LANE:kv * half + (gp + 1) * LANE] = acc[:, 0:LANE]
            lh_ref[rows, kv * half + KVW + gp * LANE:kv * half + KVW + (gp + 1) * LANE] = acc[:, LANE:2 * LANE]

    fill(0)
    for u in range(len(units)):
        if u + 1 < len(units):
            fill(u + 1)
        matmuls(u)


def _cmp_lh_prompt(kvc_t, wc):
    nbatch, _, _, _, seq = kvc_t.shape
    tt = min(seq, 2048)
    assert seq % tt == 0 and tt % LANE == 0
    nt = seq // tt
    out_w = 4 * KVW
    return pl.pallas_call(
        functools.partial(_cmp_lh_kernel, n_in=1, n_scalar=0),
        grid=(nbatch, nt),
        in_specs=[pl.BlockSpec((None, 2, N_KV, HEAD_DIM, tt), lambda b, i: (b, 0, 0, 0, i)), _const_spec(wc.shape)],
        out_specs=pl.BlockSpec((tt // CMP_STRIDE, out_w), lambda b, i: (b * nt + i, 0)),
        out_shape=jax.ShapeDtypeStruct((nbatch * seq // CMP_STRIDE, out_w), F32),
        scratch_shapes=[pltpu.VMEM((4, N_KV // 2, CMP_STRIDE * (tt // (2 * CMP_STRIDE) + 4), LANE), F32)],
        compiler_params=_params(("parallel", "parallel")), name="cmp_lh_prompt",
    )(kvc_t, wc)


def _cmp_lh_sample(pool_t, page_flat, wc):
    n_pages = page_flat.shape[0]
    pps = PAGES_PER_STEP
    assert n_pages % pps == 0
    steps = n_pages // pps
    blk = (None,) + tuple(pool_t.shape[1:])
    page_specs = [pl.BlockSpec(blk, (lambda i, pt, _p=p: (pt[i * pps + _p], 0, 0, 0, 0))) for p in range(pps)]
    nd = wc.ndim
    w_spec = pl.BlockSpec(wc.shape, lambda i, pt: (0,) * nd, pipeline_mode=pl.Buffered(1))
    out_w = 4 * KVW
    rows = pps * PAGE_SIZE // CMP_STRIDE
    grid_spec = pltpu.PrefetchScalarGridSpec(
        num_scalar_prefetch=1, grid=(steps,),
        in_specs=page_specs + [w_spec],
        out_specs=pl.BlockSpec((rows, out_w), lambda i, pt: (i, 0)),
        scratch_shapes=[pltpu.VMEM((4, N_KV // 2, CMP_STRIDE * (rows // 2 + 4), LANE), F32)])
    return pl.pallas_call(
        functools.partial(_cmp_lh_kernel, n_in=pps, n_scalar=1),
        grid_spec=grid_spec,
        out_shape=jax.ShapeDtypeStruct((steps * rows, out_w), F32),
        compiler_params=_params(("parallel",)), name="cmp_lh_sample",
    )(page_flat, *([pool_t] * pps), wc)


def _compress_finish(lh, pe_ref, w1f_ref, b1_ref, w2_ref, b2_ref, f32_v=False):
    nc = lh.shape[0]
    outs = []
    for kv in range(2):
        pe8 = jnp.broadcast_to(pe_ref[kv], (8, pe_ref.shape[-1])).astype(BF16)
        cb = _dot(pe8, w1f_ref[kv])[0:1, :] + b1_ref[kv]
        lo = lh[:, kv * 2 * KVW:kv * 2 * KVW + KVW]
        hi = lh[:, kv * 2 * KVW + KVW:(kv + 1) * 2 * KVW]
        hi_next = pltpu.roll(hi, nc - 1, axis=0)
        hid = _silu(lo + hi_next + cb)
        out = _dot(hid.astype(BF16), w2_ref[kv]) + b2_ref[kv]
        outs.append(out if (f32_v and kv == 1) else out.astype(BF16))
    return outs


def _select_blocks(imp, blk, jq, axis):
    forced = (blk == 0) | (blk == jq) | (blk == jq - 1)
    score = jnp.where((blk > jq) | (blk < 0), NEG_INF, jnp.where(forced, jnp.inf, imp))
    blk_f = blk.astype(F32)
    sel = jnp.zeros(imp.shape, F32)
    idxs = []
    for _ in range(N_SELECT):
        m = jnp.max(score, axis=axis, keepdims=True)
        idx = jnp.min(jnp.where(score == m, blk_f, 1e9), axis=axis, keepdims=True)
        hit = (blk_f == idx) & (m > NEG_INF)
        sel = jnp.where(hit, 1.0, sel)
        score = jnp.where(blk_f == idx, NEG_INF, score)
        idxs.append(idx)
    return sel, idxs


def _gate(xn, wz_ref, wgl_ref, gb_ref, e_ref):
    z = _dot(xn, wz_ref[...])
    gl = _dot(xn, wgl_ref[...]) + gb_ref[...]
    e = e_ref[...]
    a, b, c = _split3(gl)
    gle = _dot(a, e) + _dot(b, e) + _dot(c, e)
    return _silu(z), jax.nn.sigmoid(gle)


def _gate_kernel(x_ref, g_ref, wz_ref, wgl_ref, gb_ref, e_ref, sz_ref, gg_ref):
    xn = _rms(x_ref[...], g_ref[...]).astype(BF16)
    sz, gg = _gate(xn, wz_ref, wgl_ref, gb_ref, e_ref)
    sz_ref[...] = sz
    gg_ref[...] = gg


def _out_kernel(y_ref, x_ref, w_ref, g_ref, o_ref):
    o_ref[...] = _rms(_dot(y_ref[...].astype(BF16), w_ref[...]) + x_ref[...], g_ref[...])


def _attn_prompt_kernel(x_ref, qc_ref, qr_ref, lh_ref, ksa_ref, vsp_ref,
                        kw0_ref, kw1_ref, kw2_ref, vw0_ref, vw1_ref, vw2_ref,
                        g1_ref, gf_ref, wz_ref, wgl_ref, gb_ref, e_ref, wout_ref,
                        pe_ref, w1f_ref, b1_ref, w2_ref, b2_ref, ovt_ref,
                        out_ref, kc_scr, vc_scr, o_scr, *, tq, seq):
    qi = pl.program_id(1)
    t0 = qi * tq
    nc = seq // CMP_STRIDE
    rows = Q_PER_KV * tq

    @pl.when(qi == 0)
    def _():
        kc, vc = _compress_finish(lh_ref[...], pe_ref, w1f_ref, b1_ref, w2_ref, b2_ref)
        kc_scr[...] = kc
        vc_scr[...] = vc

    a_row = lax.broadcasted_iota(jnp.int32, (rows, 1), 0) % tq
    pos_row = t0 + a_row
    c_col = lax.broadcasted_iota(jnp.int32, (1, tq), 1)
    n_col = lax.broadcasted_iota(jnp.int32, (1, nc), 1)
    mask_c = (n_col * CMP_STRIDE + CMP_BLOCK - 1) <= pos_row
    causal = c_col <= a_row
    blk_t = lax.broadcasted_iota(jnp.int32, (2 * SEL_BLOCK, tq), 0) - SEL_BLOCK
    jq_t = (t0 + lax.broadcasted_iota(jnp.int32, (2 * SEL_BLOCK, tq), 1)) // SEL_BLOCK
    lane_q = lax.broadcasted_iota(jnp.int32, (tq, SLOT), 1)
    tiny = jnp.finfo(jnp.float32).tiny

    def stack(ref, g):
        return jnp.concatenate([ref[:, (Q_PER_KV * g + r) * SLOT:(Q_PER_KV * g + r + 1) * SLOT]
                                for r in range(Q_PER_KV)], axis=0)

    def put(br, g, o):
        for j in range(Q_PER_KV // 2):
            pair = jnp.concatenate([o[(2 * j) * tq:(2 * j + 1) * tq, 0:HEAD_DIM],
                                    o[(2 * j + 1) * tq:(2 * j + 2) * tq, 0:HEAD_DIM]], axis=1)
            h0 = Q_PER_KV * g + 2 * j
            o_scr[br, :, h0 * HEAD_DIM:(h0 + 2) * HEAD_DIM] = pair

    for g in range(N_KV):
        gs = slice(g * SLOT, (g + 1) * SLOT)
        s = _dot_nt(stack(qc_ref, g), kc_scr[:, gs])
        s = jnp.where(mask_c, s, NEG_INF)
        m = jnp.max(s, axis=-1, keepdims=True)
        m = jnp.where(m > NEG_INF, m, 0.0)
        e = jnp.where(mask_c, jnp.exp(s - m), 0.0)
        p = e / jnp.maximum(jnp.sum(e, axis=-1, keepdims=True), tiny)
        put(0, g, _dot(p.astype(BF16), vc_scr[:, gs]))
        psum = p[0:tq] + p[tq:2 * tq] + p[2 * tq:3 * tq] + p[3 * tq:4 * tq]
        ovt = ovt_ref[...]
        pa, pb, pc = _split3(psum)
        imp_t = _dot_nt(ovt, pa) + _dot_nt(ovt, pb) + _dot_nt(ovt, pc)
        sel_t, _ = _select_blocks(imp_t, blk_t, jq_t, axis=0)
        sel = jnp.transpose(sel_t)
        maskpart = jnp.where(lane_q >= HEAD_DIM, (sel - 1.0) * MASK_BIG, 0.0).astype(BF16)
        qr_g = stack(qr_ref, g)
        qa = jnp.where(jnp.concatenate([lane_q] * Q_PER_KV, axis=0) < HEAD_DIM, qr_g,
                       jnp.concatenate([maskpart] * Q_PER_KV, axis=0))

        def sel_tile(kt, carry, diag):
            m_i, l_i, acc = carry
            start = pl.multiple_of(kt * tq, tq)
            k_t = ksa_ref[kt, g * SLOT:(g + 1) * SLOT, :]
            v = vsp_ref[pl.ds(start, tq), gs]
            st = _dot(qa, k_t)
            if diag:
                st = jnp.where(causal, st, -MASK_BIG)
            m_new = jnp.maximum(m_i, jnp.max(st, axis=-1, keepdims=True))
            alpha = jnp.exp(m_i - m_new)
            pt = jnp.exp(st - m_new)
            l_new = alpha * l_i + jnp.sum(pt, axis=-1, keepdims=True)
            acc_new = alpha * acc + _dot(pt.astype(BF16), v)
            return m_new, l_new, acc_new

        init = (jnp.full((rows, 1), NEG_INF, F32), jnp.zeros((rows, 1), F32), jnp.zeros((rows, SLOT), F32))
        carry = lax.fori_loop(0, qi, lambda kt, c: sel_tile(kt, c, False), init)
        _, l_s, acc_s = sel_tile(qi, carry, True)
        put(1, g, acc_s / l_s)
        s0 = _dot(qr_g, kw0_ref[gs, :])
        s1 = _dot(qr_g, kw1_ref[gs, :])
        s2 = _dot(qr_g, kw2_ref[gs, :])
        s0 = jnp.where((c_col > a_row) & (qi >= 2), s0, NEG_INF)
        s1 = jnp.where(qi >= 1, s1, NEG_INF)
        s2 = jnp.where(causal, s2, NEG_INF)
        mw = jnp.maximum(jnp.maximum(jnp.max(s0, axis=-1, keepdims=True), jnp.max(s1, axis=-1, keepdims=True)),
                         jnp.max(s2, axis=-1, keepdims=True))
        p0, p1, p2 = jnp.exp(s0 - mw), jnp.exp(s1 - mw), jnp.exp(s2 - mw)
        lw = (jnp.sum(p0, axis=-1, keepdims=True) + jnp.sum(p1, axis=-1, keepdims=True)
              + jnp.sum(p2, axis=-1, keepdims=True))
        ow = (_dot(p0.astype(BF16), vw0_ref[:, gs]) + _dot(p1.astype(BF16), vw1_ref[:, gs])
              + _dot(p2.astype(BF16), vw2_ref[:, gs]))
        put(2, g, ow / lw)

    x = x_ref[...]
    xn = _rms(x, g1_ref[...]).astype(BF16)
    sz, gg = _gate(xn, wz_ref, wgl_ref, gb_ref, e_ref)
    qw = N_HEADS * HEAD_DIM
    y = None
    for br in range(N_BRANCH):
        t = o_scr[br] * sz[:, br * qw:(br + 1) * qw] * gg[:, br * qw:(br + 1) * qw]
        y = t if y is None else y + t
    out_ref[...] = _rms(_dot(y.astype(BF16), wout_ref[...]) + x, gf_ref[...])


def _attn_prompt(x1, qc, qr, lh, ksa, vsp, kwp, vwp, g1, gf, wz, wgl, gb, e_mat, wout,
                 pe, w1f, b1, w2p, b2p, ovt, *, batch, seq):
    tq = TQ
    assert seq % tq == 0 and WINDOW == 2 * tq and seq % SEL_BLOCK == 0 and seq // SEL_BLOCK <= SEL_BLOCK
    nq = seq // tq
    nc = seq // CMP_STRIDE
    d = x1.shape[1]
    qw = N_HEADS * SLOT
    pw = N_KV * SLOT
    tile = lambda w: pl.BlockSpec((tq, w), lambda b, i: (b * nq + i, 0))
    per_b = lambda rows, w: pl.BlockSpec((rows, w), lambda b, i: (b, 0), pipeline_mode=pl.Buffered(1))
    ksa_spec = pl.BlockSpec((None, nq, pw, tq), lambda b, i: (b, 0, 0, 0), pipeline_mode=pl.Buffered(1))
    kwin = lambda off: pl.BlockSpec((None, None, pw, tq), lambda b, i: (b, jnp.maximum(i - off, 0), 0, 0))
    win = lambda off: pl.BlockSpec((tq, pw), lambda b, i: (b * nq + jnp.maximum(i - off, 0), 0))
    consts = [g1[None, :], gf[None, :], wz, wgl, gb, e_mat, wout, pe, w1f, b1, w2p, b2p, ovt]
    in_specs = ([tile(d), tile(qw), tile(qw), per_b(nc, lh.shape[1]), ksa_spec, per_b(seq, pw),
                 kwin(2), kwin(1), kwin(0), win(2), win(1), win(0)]
                + [_const_spec(c.shape) for c in consts])
    return pl.pallas_call(
        functools.partial(_attn_prompt_kernel, tq=tq, seq=seq),
        grid=(batch, nq), in_specs=in_specs, out_specs=tile(d),
        out_shape=jax.ShapeDtypeStruct((batch * seq, d), F32),
        scratch_shapes=[pltpu.VMEM((nc, pw), BF16), pltpu.VMEM((nc, pw), BF16),
                        pltpu.VMEM((N_BRANCH, tq, N_HEADS * HEAD_DIM), F32)],
        compiler_params=_params(("arbitrary", "arbitrary")), name="nsa_attend_prompt",
    )(x1, qc, qr, lh, ksa, vsp, kwp, kwp, kwp, vwp, vwp, vwp, *consts)


def _attn_t_kernel(x_ref, qc_ref, qr_ref, lh_ref, ksa_ref, vst_ref,
                   kw0_ref, kw1_ref, kw2_ref, vw0_ref, vw1_ref, vw2_ref,
                   g1_ref, gf_ref, wz_ref, wgl_ref, gb_ref, wout_ref,
                   pe_ref, w1f_ref, b1_ref, w2_ref, b2_ref, ov_ref,
                   out_ref, kc_scr, vct_scr, o_scr, *, tq, seq):
    qi = pl.program_id(1)
    t0 = qi * tq
    nc = seq // CMP_STRIDE
    cols = Q_PER_KV * tq
    hd = HEAD_DIM

    @pl.when(qi == 0)
    def _():
        kc, vc = _compress_finish(lh_ref[...], pe_ref, w1f_ref, b1_ref, w2_ref, b2_ref, f32_v=True)
        kc_scr[...] = kc
        vct_scr[...] = jnp.transpose(vc).astype(BF16)

    a_col = lax.broadcasted_iota(jnp.int32, (1, cols), 1) % tq
    pos_col = t0 + a_col
    c_row = lax.broadcasted_iota(jnp.int32, (tq, 1), 0)
    n_row = lax.broadcasted_iota(jnp.int32, (nc, 1), 0)
    mask_c = (n_row * CMP_STRIDE + CMP_BLOCK - 1) <= pos_col
    causal = c_row <= a_col
    blk_t = lax.broadcasted_iota(jnp.int32, (SEL_BLOCK, tq), 0)
    jq_t = (t0 + lax.broadcasted_iota(jnp.int32, (SEL_BLOCK, tq), 1)) // SEL_BLOCK
    tiny = jnp.finfo(jnp.float32).tiny
    zpad = jnp.zeros((SLOT - hd, tq), BF16)

    def stack_t(ref, g, lower):
        return jnp.concatenate(
            [jnp.concatenate([ref[(Q_PER_KV * g + r) * hd:(Q_PER_KV * g + r + 1) * hd, :], lower], axis=0)
             for r in range(Q_PER_KV)], axis=1)

    def put(br, g, o_t):
        for r in range(Q_PER_KV):
            h = Q_PER_KV * g + r
            o_scr[br, h * hd:(h + 1) * hd, :] = o_t[:, r * tq:(r + 1) * tq]

    gsl = lambda g: slice(g * SLOT, (g + 1) * SLOT)

    def cmp_scores(g, n):
        return _dot(kc_scr[0:n, gsl(g)], stack_t(qc_ref, g, zpad))

    def cmp_finish(g, s, n):
        s = jnp.where(mask_c[0:n], s, NEG_INF)
        m = jnp.max(s, axis=0, keepdims=True)
        m = jnp.where(m > NEG_INF, m, 0.0)
        e = jnp.exp2(s - m)
        p = e * (1.0 / jnp.maximum(jnp.sum(e, axis=0, keepdims=True), tiny))
        put(0, g, _dot(vct_scr[gsl(g), 0:n], p.astype(BF16))[0:hd])
        psum = p[:, 0:tq] + p[:, tq:2 * tq] + p[:, 2 * tq:3 * tq] + p[:, 3 * tq:4 * tq]
        ov = ov_ref[:, 0:n]
        pa, pb, pc = _split3(psum)
        return _dot(ov, pa) + _dot(ov, pb) + _dot(ov, pc)


    def win_scores(g):
        qr_t = stack_t(qr_ref, g, zpad)
        return (_dot(kw0_ref[:, gsl(g)], qr_t), _dot(kw1_ref[:, gsl(g)], qr_t), _dot(kw2_ref[:, gsl(g)], qr_t))

    def win_finish(g, ss):
        s0 = jnp.where((c_row > a_col) & (qi >= 2), ss[0], NEG_INF)
        s1 = jnp.where(qi >= 1, ss[1], NEG_INF)
        s2 = jnp.where(causal, ss[2], NEG_INF)
        mw = jnp.maximum(jnp.maximum(jnp.max(s0, axis=0, keepdims=True), jnp.max(s1, axis=0, keepdims=True)),
                         jnp.max(s2, axis=0, keepdims=True))
        acc_w = (_dot(vw0_ref[gsl(g), :], jnp.exp2(s0 - mw).astype(BF16))
                 + _dot(vw1_ref[gsl(g), :], jnp.exp2(s1 - mw).astype(BF16))
                 + _dot(vw2_ref[gsl(g), :], jnp.exp2(s2 - mw).astype(BF16)))
        put(2, g, acc_w[0:hd] * (1.0 / acc_w[hd:hd + 1]))

    x = x_ref[...]
    xn_t = jnp.transpose(_rms(x, g1_ref[...])).astype(BF16)
    qw = N_HEADS * hd
    imps = [None] * N_KV
    z_raw = [None] * N_BRANCH
    s_c = cmp_scores(0, nc)
    for g in range(N_KV):
        s_next = cmp_scores(g + 1, nc) if g + 1 < N_KV else win_scores(0)
        if g < N_BRANCH:
            z_raw[g] = _dot(wz_ref[g * qw:(g + 1) * qw, :], xn_t)
        else:
            gl_raw = _dot(wgl_ref[...], xn_t)
        imps[g] = cmp_finish(g, s_c, nc)
        s_c = s_next
    gate = jax.nn.sigmoid(gl_raw + jnp.concatenate([gb_ref[...]] * (tq // LANE), axis=1))
    szs = [_silu(z) for z in z_raw]
    sel_all, _ = _select_blocks(jnp.concatenate(imps, axis=1), jnp.concatenate([blk_t] * N_KV, axis=1),
                                jnp.concatenate([jq_t] * N_KV, axis=1), axis=0)
    maskbias_all = ((sel_all - 1.0) * MASK_BIG).astype(BF16)
    s_w = s_c
    for g in range(N_KV):
        s_next = win_scores(g + 1) if g + 1 < N_KV else None
        win_finish(g, s_w)
        s_w = s_next

    qa = [stack_t(qr_ref, g, maskbias_all[:, g * tq:(g + 1) * tq]) for g in range(N_KV)]

    def scores(g, kt):
        start = pl.multiple_of(kt * tq, tq)
        return _dot(ksa_ref[pl.ds(start, tq), gsl(g)], qa[g])

    def consume(g, kt, st, m_i, acc):
        m_new = jnp.maximum(m_i, jnp.max(st, axis=0, keepdims=True))
        alpha = jnp.exp2(m_i - m_new)
        pt = jnp.exp2(st - m_new).astype(BF16)
        return m_new, alpha * acc + _dot(vst_ref[kt, gsl(g), :], pt)

    def all_groups(kt, carry, diag):
        out = []
        st = scores(0, kt)
        for g in range(N_KV):
            st_next = scores(g + 1, kt) if g + 1 < N_KV else None
            if diag:
                st = jnp.where(causal, st, -MASK_BIG)
            out.extend(consume(g, kt, st, carry[2 * g], carry[2 * g + 1]))
            st = st_next
        return tuple(out)

    init = (jnp.full((1, cols), NEG_INF, F32), jnp.zeros((SLOT, cols), F32)) * N_KV
    carry = lax.fori_loop(0, qi // 2,
                          lambda j, c: all_groups(2 * j + 1, all_groups(2 * j, c, False), False), init)
    carry = lax.cond(qi % 2 == 1, lambda c: all_groups(qi - 1, c, False), lambda c: c, carry)
    carry = all_groups(qi, carry, True)
    for g in range(N_KV):
        acc_s = carry[2 * g + 1]
        put(1, g, acc_s[0:hd] * (1.0 / acc_s[hd:hd + 1]))

    y_t = None
    for br in range(N_BRANCH):
        sz = szs[br]
        parts = []
        for h in range(N_HEADS):
            j = br * N_HEADS + h
            parts.append(o_scr[br, h * hd:(h + 1) * hd, :] * sz[h * hd:(h + 1) * hd, :] * gate[j:j + 1, :])
        t = jnp.concatenate(parts, axis=0)
        y_t = t if y_t is None else y_t + t
    o = jnp.transpose(_dot(wout_ref[...], y_t.astype(BF16)))
    out_ref[...] = _rms(o + x, gf_ref[...])


def _attn_prompt_t(x1, qc_t, qr_t, lh, ksa, vst, kwp, vwt, g1, gf, wz_t, wgl_t, gb_b, wout_t,
                   pe, w1f, b1, w2p, b2p, ov, *, batch, seq):
    tq = TQ
    assert seq % tq == 0 and WINDOW == 2 * tq and seq % SEL_BLOCK == 0 and seq // SEL_BLOCK <= SEL_BLOCK
    nq = seq // tq
    nc = seq // CMP_STRIDE
    d = x1.shape[1]
    pw = N_KV * SLOT
    qd = N_HEADS * HEAD_DIM
    tile = lambda w: pl.BlockSpec((tq, w), lambda b, i: (b * nq + i, 0))
    tile_t = lambda rows: pl.BlockSpec((None, None, rows, tq), lambda b, i: (b, i, 0, 0))
    per_b = lambda rows, w: pl.BlockSpec((rows, w), lambda b, i: (b, 0), pipeline_mode=pl.Buffered(1))
    vst_spec = pl.BlockSpec((None, nq, pw, tq), lambda b, i: (b, 0, 0, 0), pipeline_mode=pl.Buffered(1))
    kwin = lambda off: pl.BlockSpec((tq, pw), lambda b, i: (b * nq + jnp.maximum(i - off, 0), 0))
    vwin = lambda off: pl.BlockSpec((None, None, pw, tq), lambda b, i: (b, jnp.maximum(i - off, 0), 0, 0))
    consts = [g1[None, :], gf[None, :], wz_t, wgl_t, gb_b, wout_t, pe, w1f, b1, w2p, b2p, ov]
    in_specs = ([tile(d), tile_t(qd), tile_t(qd), per_b(nc, lh.shape[1]), per_b(seq, pw), vst_spec,
                 kwin(2), kwin(1), kwin(0), vwin(2), vwin(1), vwin(0)]
                + [_const_spec(c.shape) for c in consts])
    return pl.pallas_call(
        functools.partial(_attn_t_kernel, tq=tq, seq=seq),
        grid=(batch, nq), in_specs=in_specs, out_specs=tile(d),
        out_shape=jax.ShapeDtypeStruct((batch * seq, d), F32),
        scratch_shapes=[pltpu.VMEM((nc, pw), BF16), pltpu.VMEM((pw, nc), BF16),
                        pltpu.VMEM((N_BRANCH, qd, tq), F32)],
        compiler_params=_params(("arbitrary", "arbitrary")), name="nsa_attend_prompt",
    )(x1, qc_t, qr_t, lh, ksa, vst, kwp, kwp, kwp, vwt, vwt, vwt, *consts)


def _sample_cmp_kernel(lh_ref, qc_ref, pe_ref, w1f_ref, b1_ref, w2_ref, b2_ref, ov_ref,
                       imp_ref, oc_ref, *, past):
    kc, vc = _compress_finish(lh_ref[...], pe_ref, w1f_ref, b1_ref, w2_ref, b2_ref)
    nc = kc.shape[0]
    nbp = ov_ref.shape[1]
    q = qc_ref[...]
    head_row = lax.broadcasted_iota(jnp.int32, (N_HEADS, 1), 0)
    n_col = lax.broadcasted_iota(jnp.int32, (1, nc), 1)
    mask_c = (n_col * CMP_STRIDE + CMP_BLOCK - 1) <= past
    row8 = lax.broadcasted_iota(jnp.int32, (8, 1), 0)
    tiny = jnp.finfo(jnp.float32).tiny
    oc = jnp.zeros((N_HEADS, SLOT), F32)
    psum8 = jnp.zeros((8, nc), F32)
    for g in range(N_KV):
        gs = slice(g * SLOT, (g + 1) * SLOT)
        in_g = (head_row // Q_PER_KV) == g
        s = jnp.where(mask_c, _dot_nt(q, kc[:, gs]), NEG_INF)
        m = jnp.max(s, axis=-1, keepdims=True)
        m = jnp.where(m > NEG_INF, m, 0.0)
        e = jnp.where(mask_c, jnp.exp(s - m), 0.0)
        p = e / jnp.maximum(jnp.sum(e, axis=-1, keepdims=True), tiny)
        oc = jnp.where(in_g, _dot(p.astype(BF16), vc[:, gs]), oc)
        pg = jnp.sum(jnp.where(in_g, p, 0.0), axis=0, keepdims=True)
        psum8 = jnp.where(row8 == g, pg, psum8)
    ov = ov_ref[...]
    pa, pb, pc = _split3(psum8)
    imp_ref[...] = _dot(pa, ov) + _dot(pb, ov) + _dot(pc, ov)
    oc_ref[...] = oc


def _sample_topk_kernel(imp_ref, topi_ref, *, past):
    imp_t = jnp.transpose(imp_ref[...])
    blk = lax.broadcasted_iota(jnp.int32, imp_t.shape, 0)
    blk = jnp.where(blk < past // SEL_BLOCK + 1, blk, -1)
    _, idxs = _select_blocks(imp_t, blk, jnp.int32(past // SEL_BLOCK), axis=0)
    topi_ref[...] = jnp.concatenate(idxs, axis=0).astype(jnp.int32)


def _sample_cmp(lh3, qc3, pe, w1f, b1, w2p, b2p, ov, *, past):
    nb, nc, w = lh3.shape
    nbp = ov.shape[1]
    consts = [pe, w1f, b1, w2p, b2p, ov]
    imp, oc = pl.pallas_call(
        functools.partial(_sample_cmp_kernel, past=past),
        grid=(nb,),
        in_specs=[pl.BlockSpec((None, nc, w), lambda b: (b, 0, 0)),
                  pl.BlockSpec((None, N_HEADS, SLOT), lambda b: (b, 0, 0))]
                 + [_const_spec(c.shape) for c in consts],
        out_specs=(pl.BlockSpec((None, 8, nbp), lambda b: (b, 0, 0)),
                   pl.BlockSpec((None, N_HEADS, SLOT), lambda b: (b, 0, 0))),
        out_shape=(jax.ShapeDtypeStruct((nb, 8, nbp), F32),
                   jax.ShapeDtypeStruct((nb, N_HEADS, SLOT), F32)),
        compiler_params=_params(("parallel",)), name="nsa_sample_cmp",
    )(lh3, qc3, *consts)
    rows = nb * 8
    topi = pl.pallas_call(
        functools.partial(_sample_topk_kernel, past=past), grid=(1,),
        in_specs=[_const_spec((rows, nbp), single=False)],
        out_specs=_const_spec((N_SELECT, rows), single=False),
        out_shape=jax.ShapeDtypeStruct((N_SELECT, rows), jnp.int32),
        compiler_params=_params(("arbitrary",)), name="nsa_sample_topk",
    )(imp.reshape(rows, nbp))
    topi = jnp.transpose(topi.reshape(N_SELECT, nb, 8)[:, :, :N_KV], (1, 2, 0))
    return topi, oc


def _sample_attn_kernel(pt_ref, ti_ref, *refs, n_pool_blk, wlen):
    kv_refs = refs[:N_SELECT]
    (qr_ref, ksn_ref, vsn_ref, kwn_ref, vwn_ref, win_ref, wnew_ref, oc_ref, sz_ref, gg_ref,
     y_ref, wout_ref) = refs[N_SELECT:]
    b = pl.program_id(0)
    g = pl.program_id(1)
    per_page = PAGE_SIZE // SEL_BLOCK
    q = qr_ref[...]
    qf = q[:, 0:HEAD_DIM].astype(F32)
    in_g = (lax.broadcasted_iota(jnp.int32, (N_HEADS, 1), 0) // Q_PER_KV) == g
    zpad = jnp.zeros((SLOT - HEAD_DIM, PAGE_SIZE), BF16)
    tok_half = lax.broadcasted_iota(jnp.int32, (1, PAGE_SIZE), 1) // SEL_BLOCK

    s_new = jnp.sum(qf * ksn_ref[...], axis=-1, keepdims=True)
    scores = []
    m = s_new
    for r in range(N_SELECT):
        t = ti_ref[(b * N_KV + g) * N_SELECT + r]
        ok = (tok_half == t % per_page) & (t < n_pool_blk)
        k_t = jnp.concatenate([kv_refs[r][0].astype(BF16), zpad], axis=0)
        s = jnp.where(ok, _dot(q, k_t), NEG_INF)
        scores.append(s)
        m = jnp.maximum(m, jnp.max(s, axis=-1, keepdims=True))
    p_new = jnp.exp(s_new - m)
    l = p_new
    o = p_new * vsn_ref[...]
    for r in range(N_SELECT):
        p = jnp.exp(scores[r] - m)
        l = l + jnp.sum(p, axis=-1, keepdims=True)
        o = o + _dot_nt(p.astype(BF16), kv_refs[r][1].astype(BF16))
    o_s = o / l

    wkey = lax.broadcasted_iota(jnp.int32, (1, wlen), 1)
    kw_t = jnp.concatenate([win_ref[0].astype(BF16), jnp.zeros((SLOT - HEAD_DIM, wlen), BF16)], axis=0)
    s = jnp.where(wkey > wlen - WINDOW, _dot(q, kw_t), NEG_INF)
    s_new = jnp.sum(qf * kwn_ref[...], axis=-1, keepdims=True)
    m = jnp.maximum(jnp.max(s, axis=-1, keepdims=True), s_new)
    p = jnp.exp(s - m)
    p_new = jnp.exp(s_new - m)
    l = jnp.sum(p, axis=-1, keepdims=True) + p_new
    o_w = (_dot_nt(p.astype(BF16), win_ref[1].astype(BF16)) + p_new * vwn_ref[...]) / l

    o_c = oc_ref[:, 0:HEAD_DIM]
    y = o_c * sz_ref[0] * gg_ref[0] + o_s * sz_ref[1] * gg_ref[1] + o_w * sz_ref[2] * gg_ref[2]

    @pl.when(g == 0)
    def _():
        y_ref[...] = jnp.zeros(y_ref.shape, F32)

    y_ref[...] = jnp.where(in_g, y, y_ref[...])
    for kv in range(2):
        shifted = pltpu.roll(win_ref[kv], wlen - 1, axis=1)
        wout_ref[kv] = jnp.where(wkey == wlen - 1, wnew_ref[kv], shifted)


def _sample_attn(page_flat, topi_flat, pool_t, qr3, ksn, vsn, kwn, vwn, win_t, wnew, oc, sz4, gg4, *, n_pages):
    nb = qr3.shape[0]
    wlen = win_t.shape[-1]
    per_page = PAGE_SIZE // SEL_BLOCK
    n_pool_blk = n_pages * per_page

    def pool_map(r):
        def f(b, g, pt, ti):
            blk = jnp.minimum(ti[(b * N_KV + g) * N_SELECT + r], n_pool_blk - 1)
            return (pt[b * n_pages + blk // per_page], 0, g, 0, 0)
        return f

    b3 = lambda s1, s2: pl.BlockSpec((None, s1, s2), lambda b, g, pt, ti: (b, 0, 0))
    b4 = pl.BlockSpec((None, N_BRANCH, N_HEADS, HEAD_DIM), lambda b, g, pt, ti: (b, 0, 0, 0))
    page = lambda r: pl.BlockSpec((None, 2, None, HEAD_DIM, PAGE_SIZE), pool_map(r))
    wspec = lambda last: pl.BlockSpec((None, 2, None, HEAD_DIM, last), lambda b, g, pt, ti: (b, 0, g, 0, 0))
    in_specs = ([page(r) for r in range(N_SELECT)]
                + [b3(N_HEADS, SLOT)] + [b3(N_HEADS, HEAD_DIM)] * 4
                + [wspec(wlen), wspec(1), b3(N_HEADS, SLOT), b4, b4])
    grid_spec = pltpu.PrefetchScalarGridSpec(
        num_scalar_prefetch=2, grid=(nb, N_KV), in_specs=in_specs,
        out_specs=(b3(N_HEADS, HEAD_DIM), wspec(wlen)))
    return pl.pallas_call(
        functools.partial(_sample_attn_kernel, n_pool_blk=n_pool_blk, wlen=wlen),
        grid_spec=grid_spec,
        out_shape=(jax.ShapeDtypeStruct((nb, N_HEADS, HEAD_DIM), F32),
                   jax.ShapeDtypeStruct(win_t.shape, F32)),
        compiler_params=_params(("arbitrary", "arbitrary")), name="nsa_sample_sel_win",
    )(page_flat, topi_flat, *([pool_t] * N_SELECT), qr3, ksn, vsn, kwn, vwn, win_t, wnew, oc, sz4, gg4)


def _rope_tables(pos):
    half = ROT_DIM // 2
    freqs = jnp.exp(-math.log(ROPE_THETA) * jnp.arange(half, dtype=F32) * (2.0 / ROT_DIM))
    ang = pos.astype(F32)[:, None] * freqs[None, :]
    cos, sin = jnp.cos(ang), jnp.sin(ang)
    n = pos.shape[0]
    one = jnp.ones((n, HEAD_DIM - ROT_DIM), F32)
    zero8 = jnp.zeros((n, half), F32)
    zrest = jnp.zeros((n, HEAD_DIM - ROT_DIM), F32)
    c64 = jnp.concatenate([cos, cos, one], axis=1)
    s1_64 = jnp.concatenate([-sin, zero8, zrest], axis=1)
    s2_64 = jnp.concatenate([zero8, sin, zrest], axis=1)
    pad1 = jnp.ones((n, SLOT - HEAD_DIM), F32)
    pad0 = jnp.zeros((n, SLOT - HEAD_DIM), F32)
    tabs_q = (jnp.concatenate([c64, pad1], axis=1), jnp.concatenate([s1_64, pad0], axis=1),
              jnp.concatenate([s2_64, pad0], axis=1))
    tabs_k = tuple(jnp.concatenate([t, t], axis=1) for t in (c64, s1_64, s2_64))
    tabs_t = (jnp.transpose(cos), jnp.transpose(sin), jnp.transpose(cos))
    return tabs_q, tabs_k, tabs_t


def _pad_heads(w, n_heads):
    k = w.shape[0]
    w3 = w.reshape(k, n_heads, HEAD_DIM)
    return jnp.pad(w3, ((0, 0), (0, 0), (0, SLOT - HEAD_DIM))).reshape(k, n_heads * SLOT)


def _overlap(nc, nb):
    n = np.arange(nc)[:, None] * CMP_STRIDE
    j = np.arange(nb)[None, :] * SEL_BLOCK
    return ((n <= j + SEL_BLOCK - 1) & (n + CMP_BLOCK - 1 >= j)).astype(np.float32)


def kernel(x_prompt, x_sample, cache_cmp_kv, cache_sel_kv, state_win_kv, page_table, norm_g, final_norm_g,
           a_w_in, a_ln_g, a_ln_b, a_w_s, a_b_s, a_w_out, b_w_in, b_cmp_pe, b_cmp_w1, b_cmp_b1, b_cmp_w2,
           b_cmp_b2, b_gate_b, b_w_out):
    batch, seq, d = x_prompt.shape
    nb, dec_seq, _ = x_sample.shape
    assert dec_seq == 1
    n_pages = page_table.shape[1]
    past = n_pages * PAGE_SIZE
    assert past % SEL_BLOCK == 0 and past % CMP_STRIDE == 0 and past // SEL_BLOCK + 1 >= N_SELECT
    wlen = state_win_kv.shape[2]
    qw = N_HEADS * HEAD_DIM

    a_win = a_w_in[0].astype(BF16)
    a_wout = a_w_out[0].astype(BF16)
    xp1 = _layer_a(x_prompt.reshape(batch * seq, d), norm_g[0], a_win, a_ln_g[0], a_ln_b[0], a_w_s[0], a_b_s[0],
                   a_wout, sample=False)
    xs1, chunk_v = _layer_a(x_sample.reshape(nb, d), norm_g[0], a_win, a_ln_g[0], a_ln_b[0], a_w_s[0], a_b_s[0],
                            a_wout, sample=True)

    w_in = b_w_in[0]
    o1 = qw
    o2 = o1 + N_BRANCH * KVROW
    o3 = o2 + N_BRANCH * qw
    wq_pad = _pad_heads(w_in[:, :o1], N_HEADS).astype(BF16)
    wkv = w_in[:, o1:o2].astype(BF16)
    wk_sel = w_in[:, o1 + KVROW:o1 + KVROW + KVW]
    wk_win = w_in[:, o1 + 2 * KVROW:o1 + 2 * KVROW + KVW]
    wk_pad = jnp.concatenate([_pad_heads(w, N_KV) for w in (wk_sel, wk_win)], axis=1).astype(BF16)
    wz = w_in[:, o2:o3].astype(BF16)
    n_gate = N_BRANCH * N_HEADS
    n_gate_pad = -(-n_gate // 16) * 16
    wgl = jnp.pad(w_in[:, o3:], ((0, 0), (0, LANE - n_gate))).astype(BF16)
    gb = jnp.pad(b_gate_b[0].reshape(1, n_gate), ((0, 0), (0, LANE - n_gate)))
    wz_t = jnp.transpose(w_in[:, o2:o3]).astype(BF16)
    wgl_t = jnp.pad(jnp.transpose(w_in[:, o3:]), ((0, n_gate_pad - n_gate), (0, 0))).astype(BF16)
    gb_b = jnp.broadcast_to(jnp.pad(b_gate_b[0].reshape(n_gate), (0, n_gate_pad - n_gate))[:, None],
                            (n_gate_pad, LANE))
    wout_t = jnp.transpose(b_w_out[0]).astype(BF16)
    e_np = np.zeros((LANE, n_gate * HEAD_DIM), np.float32)
    for j in range(n_gate):
        e_np[j, j * HEAD_DIM:(j + 1) * HEAD_DIM] = 1.0
    e_mat = jnp.asarray(e_np, dtype=BF16)
    wout = b_w_out[0].astype(BF16)
    w1 = b_cmp_w1[0]
    eye = jnp.eye(N_KV, dtype=F32)
    w1r = w1.reshape(2, 2, CMP_STRIDE, HEAD_DIM, HEAD_DIM)
    wc = jnp.einsum("Gg,kljch->kjGclgh", jnp.eye(2, dtype=F32), w1r)
    wc = wc.reshape(2, CMP_STRIDE * LANE, 2 * LANE).astype(BF16)
    pe = b_cmp_pe[0].reshape(2, 1, CMP_BLOCK * HEAD_DIM)
    w1f = jnp.tile(w1.reshape(2, CMP_BLOCK * HEAD_DIM, HEAD_DIM), (1, 1, N_KV)).astype(BF16)
    b1 = jnp.tile(b_cmp_b1[0], (1, N_KV)).reshape(2, 1, KVW)
    w2bd = jnp.einsum("Gg,kch->kGcgh", eye, b_cmp_w2[0])
    w2p = jnp.pad(w2bd, ((0, 0),) * 4 + ((0, SLOT - HEAD_DIM),)).reshape(2, KVW, N_KV * SLOT).astype(BF16)
    b2p = jnp.pad(jnp.broadcast_to(b_cmp_b2[0][:, None, :], (2, N_KV, HEAD_DIM)),
                  ((0, 0), (0, 0), (0, SLOT - HEAD_DIM))).reshape(2, 1, N_KV * SLOT)

    tabs_q, _, tabs_t = _rope_tables(jnp.arange(seq, dtype=jnp.int32))
    wq_t = jnp.transpose(w_in[:, :o1]).astype(BF16)
    wkv_t = jnp.transpose(w_in[:, o1:o2]).astype(BF16)
    (qc_t, qr_t, kvc_t, kvs_t, kvw_t, ksa, kwp, vst, vwt) = _project(xp1, norm_g[1], wq_t, wkv_t, wk_pad, tabs_q,
                                                                    tabs_t, prompt=True, seq=seq)
    fm = lambda t: t.reshape(t.shape[0], 2, N_KV, HEAD_DIM, t.shape[-1])
    lh_p = _cmp_lh_prompt(fm(kvc_t), wc)
    ov_t = np.zeros((SEL_BLOCK, seq // CMP_STRIDE), np.float32)
    ov_t[:seq // SEL_BLOCK] = _overlap(seq // CMP_STRIDE, seq // SEL_BLOCK).T
    y_prompt = _attn_prompt_t(xp1, qc_t, qr_t, lh_p, ksa, vst, kwp, vwt, norm_g[1], final_norm_g, wz_t, wgl_t, gb_b,
                              wout_t, pe, w1f, b1, w2p, b2p, jnp.asarray(ov_t, dtype=BF16), batch=batch, seq=seq)

    pos_s = jnp.full((nb,), past, dtype=jnp.int32)
    tabs_qs, tabs_ks, _ = _rope_tables(pos_s)
    qc_s, qr_s, kvc_s, kvs_s, kvw_s = _project(xs1, norm_g[1], wq_pad, wkv, None, tabs_qs, tabs_ks,
                                               prompt=False, seq=1)
    page_flat = page_table.reshape(-1).astype(jnp.int32)
    n_phys = cache_cmp_kv.shape[1]
    to_fm = lambda t: jnp.transpose(t, (0, 2, 3, 4, 1))
    lh_s = _cmp_lh_sample(to_fm(cache_cmp_kv[0]), page_flat, wc)
    nc_s = past // CMP_STRIDE
    nb_blk = past // SEL_BLOCK + 1
    nbp = -(-nb_blk // LANE) * LANE
    ov_np = np.zeros((nc_s, nbp), np.float32)
    ov_np[:, :nb_blk] = _overlap(nc_s, nb_blk)
    topi, oc_s = _sample_cmp(lh_s.reshape(nb, nc_s, lh_s.shape[1]), qc_s.reshape(nb, N_HEADS, SLOT),
                             pe, w1f, b1, w2p, b2p, jnp.asarray(ov_np, dtype=BF16), past=past)
    topi_flat = topi.reshape(-1)
    r_gate = xs1.shape[0]
    sz_s, gg_s = pl.pallas_call(
        _gate_kernel, grid=(1,),
        in_specs=[_const_spec(s, single=False) for s in
                  ((r_gate, d), (1, d), wz.shape, wgl.shape, gb.shape, e_mat.shape)],
        out_specs=(_const_spec((r_gate, N_BRANCH * qw), single=False),) * 2,
        out_shape=(jax.ShapeDtypeStruct((r_gate, N_BRANCH * qw), F32),) * 2,
        compiler_params=_params(("arbitrary",)), name="nsa_gate_sample",
    )(xs1, norm_g[1][None, :], wz, wgl, gb, e_mat)
    expand = lambda t: jnp.repeat(t.reshape(nb, N_KV, HEAD_DIM), Q_PER_KV, axis=1)
    y_heads, win_s_t = _sample_attn(
        page_flat, topi_flat, to_fm(cache_sel_kv[0]), qr_s.reshape(nb, N_HEADS, SLOT),
        expand(kvs_s[:, :KVW]), expand(kvs_s[:, KVW:]), expand(kvw_s[:, :KVW]), expand(kvw_s[:, KVW:]),
        to_fm(state_win_kv[0]), kvw_s.reshape(nb, 2, N_KV, HEAD_DIM, 1), oc_s,
        sz_s.reshape(nb, N_BRANCH, N_HEADS, HEAD_DIM), gg_s.reshape(nb, N_BRANCH, N_HEADS, HEAD_DIM),
        n_pages=n_pages)
    y_sample = pl.pallas_call(
        _out_kernel, grid=(1,),
        in_specs=[_const_spec(s, single=False) for s in ((nb, qw), (nb, d), wout.shape, (1, d))],
        out_specs=_const_spec((nb, d), single=False),
        out_shape=jax.ShapeDtypeStruct((nb, d), F32),
        compiler_params=_params(("arbitrary",)), name="nsa_out_sample",
    )(y_heads.reshape(nb, qw), xs1, wout, final_norm_g[None, :])

    kv6 = lambda t: t.reshape(1, nb, 1, 2, N_KV, HEAD_DIM)
    from_fm = lambda t: jnp.transpose(t, (0, 4, 1, 2, 3))[None]
    wl_p = min(WINDOW, seq)
    return (y_prompt.reshape(batch, seq, d), y_sample.reshape(nb, 1, d),
            from_fm(fm(kvc_t)), kv6(kvc_s), from_fm(fm(kvs_t)), kv6(kvs_s),
            from_fm(fm(kvw_t)[..., seq - wl_p:]), from_fm(win_s_t), chunk_v.reshape(1, nb, 1, -1))
```

```python
import functools
import math

import numpy as np
import jax
import jax.numpy as jnp
from jax import lax
from jax.experimental import pallas as pl
from jax.experimental.pallas import tpu as pltpu

F32 = jnp.float32
BF16 = jnp.bfloat16
EPS = 1e-6

A_CHUNK = 128
A_GROUPS = 16
N_HEADS = 16
HEAD_DIM = 64
N_KV = 4
Q_PER_KV = N_HEADS // N_KV
ROT_DIM = HEAD_DIM // 4
ROPE_THETA = 500000.0
CMP_BLOCK = 32
CMP_STRIDE = 16
SEL_BLOCK = 64
N_SELECT = 16
WINDOW = 512
N_BRANCH = 3
PAGE_SIZE = 128

LANE = 128
SLOT = LANE
KVW = N_KV * HEAD_DIM
KVROW = 2 * KVW
TM_ROWS = 256
TQ = 256
PAGES_PER_STEP = 32
VMEM_LIMIT = 56 * 1024 * 1024
MASK_BIG = float(2.0 ** 127)
SQRT_HALF = float(np.sqrt(0.5))
LOG2E = float(1.0 / np.log(2.0))
NEG_INF = float("-inf")


def _rms(x, g):
    return x * lax.rsqrt(jnp.mean(x * x, axis=-1, keepdims=True) + EPS) * g


def _gelu(x):
    return 0.5 * x * (1.0 + lax.erf(x * SQRT_HALF))


def _silu(x):
    return x * jax.nn.sigmoid(x)


def _dot(a, b):
    return jnp.dot(a, b, preferred_element_type=F32)


def _dot_nt(a, b):
    return lax.dot_general(a, b, (((1,), (1,)), ((), ())), preferred_element_type=F32)


def _split3(x):
    a = x.astype(BF16)
    r = x - a.astype(F32)
    b = r.astype(BF16)
    c = (r - b.astype(F32)).astype(BF16)
    return a, b, c


def _rope(a, c_tab, s1_tab, s2_tab):
    w = a.shape[-1]
    reps = w // LANE
    ct = jnp.concatenate([c_tab] * reps, axis=1) if reps > 1 else c_tab
    s1 = jnp.concatenate([s1_tab] * reps, axis=1) if reps > 1 else s1_tab
    s2 = jnp.concatenate([s2_tab] * reps, axis=1) if reps > 1 else s2_tab
    half = ROT_DIM // 2
    return a * ct + pltpu.roll(a, w - half, axis=1) * s1 + pltpu.roll(a, half, axis=1) * s2


def _const_spec(shape, single=True):
    nd = len(shape)
    kw = {"pipeline_mode": pl.Buffered(1)} if single else {}
    return pl.BlockSpec(tuple(shape), lambda *a, _nd=nd: (0,) * _nd, **kw)


def _params(sem):
    return pltpu.CompilerParams(dimension_semantics=sem, vmem_limit_bytes=VMEM_LIMIT)


def _layer_a_kernel(x_ref, g_ref, win_ref, lng_ref, lnb_ref, ws_ref, bs_ref, wout_ref, *rest, sample, tm, aw):
    if sample:
        xo_ref, v_ref = rest
    else:
        xo_ref, mix_ref = rest
    x = x_ref[...]
    xn = _rms(x, g_ref[...]).astype(BF16)
    v = _gelu(_dot(xn, win_ref[:, aw:2 * aw]))
    vc = v - jnp.mean(v, axis=-1, keepdims=True)
    v = vc * lax.rsqrt(jnp.mean(vc * vc, axis=-1, keepdims=True) + EPS) * lng_ref[...] + lnb_ref[...]
    if sample:
        v_ref[...] = v
        mix = v * ws_ref[...] + bs_ref[...]
    else:
        vb = v.astype(BF16)
        tri = (lax.broadcasted_iota(jnp.int32, (A_CHUNK, A_CHUNK), 0)
               >= lax.broadcasted_iota(jnp.int32, (A_CHUNK, A_CHUNK), 1))
        for g in range(A_GROUPS):
            wm = jnp.where(tri, ws_ref[g], 0.0).astype(BF16)
            bias = bs_ref[:, g:g + 1]
            for c in range(tm // A_CHUNK):
                blk = vb[c * A_CHUNK:(c + 1) * A_CHUNK, g * A_CHUNK:(g + 1) * A_CHUNK]
                mix_ref[c * A_CHUNK:(c + 1) * A_CHUNK, g * A_CHUNK:(g + 1) * A_CHUNK] = _dot(wm, blk) + bias
        mix = mix_ref[...]
    u = _gelu(_dot(xn, win_ref[:, 0:aw]))
    z = _dot(xn, win_ref[:, 2 * aw:3 * aw])
    y = (u * mix * _silu(z)).astype(BF16)
    xo_ref[...] = _dot(y, wout_ref[...]) + x


def _layer_a(x2d, norm_g, w_in, ln_g, ln_b, w_s, b_s, w_out, *, sample):
    r, d = x2d.shape
    aw = w_out.shape[0]
    tm = r if sample else TM_ROWS
    assert r % tm == 0 and tm % A_CHUNK == 0 or sample
    if sample:
        ws_arg = jnp.repeat(w_s[:, 0, 0], A_CHUNK)[None, :]
        bs_arg = jnp.repeat(b_s[:, 0], A_CHUNK)[None, :]
    else:
        ws_arg = w_s
        bs_arg = b_s.T
    row = pl.BlockSpec((tm, d), lambda i: (i, 0))
    in_specs = [row, _const_spec((1, d)), _const_spec(w_in.shape), _const_spec((1, aw)), _const_spec((1, aw)),
                _const_spec(ws_arg.shape), _const_spec(bs_arg.shape), _const_spec(w_out.shape)]
    if sample:
        out_shape = (jax.ShapeDtypeStruct((r, d), F32), jax.ShapeDtypeStruct((r, aw), F32))
        out_specs = (row, pl.BlockSpec((tm, aw), lambda i: (i, 0)))
        scratch = []
    else:
        out_shape = jax.ShapeDtypeStruct((r, d), F32)
        out_specs = row
        scratch = [pltpu.VMEM((tm, aw), F32)]
    return pl.pallas_call(
        functools.partial(_layer_a_kernel, sample=sample, tm=tm, aw=aw),
        grid=(r // tm,), in_specs=in_specs, out_specs=out_specs, out_shape=out_shape,
        scratch_shapes=scratch, compiler_params=_params(("parallel",)),
        name="layer_a_sample" if sample else "layer_a_prompt",
    )(x2d, norm_g[None, :], w_in, ln_g[None, :], ln_b[None, :], ws_arg, bs_arg, w_out)


def _proj_kernel(x_ref, g_ref, wq_ref, wkv_ref, *rest, prompt, tm, seq):
    if prompt:
        (wkp_ref, cq_ref, s1q_ref, s2q_ref, ck_ref, s1k_ref, s2k_ref,
         qc_ref, qr_ref, kvc_ref, kvs_ref, kvw_ref, ksa_ref, kwp_ref, vst_ref, vwt_ref) = rest
    else:
        (cq_ref, s1q_ref, s2q_ref, ck_ref, s1k_ref, s2k_ref,
         qc_ref, qr_ref, kvc_ref, kvs_ref, kvw_ref) = rest
    scale = HEAD_DIM ** -0.5
    xn_f = _rms(x_ref[...], g_ref[...])
    xn = xn_f.astype(BF16)
    cq, s1q, s2q = cq_ref[...], s1q_ref[...], s2q_ref[...]
    if not prompt:
        ck, s1k, s2k = ck_ref[...], s1k_ref[...], s2k_ref[...]
        kv = _dot(xn, wkv_ref[...])
        q = _dot(xn, wq_ref[...])
        qc_ref[...] = (q * scale).astype(BF16)
        qr_ref[...] = (_rope(q, cq, s1q, s2q) * scale).astype(BF16)
        kvc_ref[...] = kv[:, 0:KVROW]
        kvs_ref[:, 0:KVW] = _rope(kv[:, KVROW:KVROW + KVW], ck, s1k, s2k)
        kvs_ref[:, KVW:KVROW] = kv[:, KVROW + KVW:2 * KVROW]
        kvw_ref[:, 0:KVW] = _rope(kv[:, 2 * KVROW:2 * KVROW + KVW], ck, s1k, s2k)
        kvw_ref[:, KVW:KVROW] = kv[:, 2 * KVROW + KVW:3 * KVROW]
        return
    xn_t = jnp.transpose(xn_f).astype(BF16)
    cos_t, sin_t = ck_ref[...], s1k_ref[...]
    half = ROT_DIM // 2

    def rope_t(a):
        parts = []
        for h in range(a.shape[0] // HEAD_DIM):
            r0 = h * HEAD_DIM
            x1, x2 = a[r0:r0 + half], a[r0 + half:r0 + 2 * half]
            parts += [x1 * cos_t - x2 * sin_t, x2 * cos_t + x1 * sin_t, a[r0 + 2 * half:r0 + HEAD_DIM]]
        return jnp.concatenate(parts, axis=0)

    q_t = _dot(wq_ref[...], xn_t)
    qc_ref[...] = (q_t * (scale * LOG2E)).astype(BF16)
    qr_ref[...] = (rope_t(q_t) * (scale * LOG2E)).astype(BF16)
    kv_t = _dot(wkv_ref[...], xn_t)
    kvc_ref[...] = kv_t[0:KVROW]
    kvs_t = jnp.concatenate([rope_t(kv_t[KVROW:KVROW + KVW]), kv_t[KVROW + KVW:2 * KVROW]], axis=0)
    kvw_t = jnp.concatenate([rope_t(kv_t[2 * KVROW:2 * KVROW + KVW]), kv_t[2 * KVROW + KVW:3 * KVROW]], axis=0)
    kvs_ref[...] = kvs_t
    kvw_ref[...] = kvw_t
    ones_pad = jnp.where(lax.broadcasted_iota(jnp.int32, (SLOT - HEAD_DIM, tm), 0) == 0, 1.0, 0.0)
    for g in range(N_KV):
        r0 = KVW + g * HEAD_DIM
        vst_ref[g * SLOT:(g + 1) * SLOT, :] = jnp.concatenate([kvs_t[r0:r0 + HEAD_DIM], ones_pad], axis=0).astype(BF16)
        vwt_ref[g * SLOT:(g + 1) * SLOT, :] = jnp.concatenate([kvw_t[r0:r0 + HEAD_DIM], ones_pad], axis=0).astype(BF16)
    pw = N_KV * SLOT
    kp = _dot(xn, wkp_ref[...])
    lane = lax.broadcasted_iota(jnp.int32, (tm, pw), 1) % SLOT
    pos = (pl.program_id(0) % (seq // tm)) * tm + lax.broadcasted_iota(jnp.int32, (tm, pw), 0)
    onehot = jnp.where(pos // SEL_BLOCK == lane - HEAD_DIM, 1.0, 0.0)
    ksa_ref[...] = jnp.where(lane < HEAD_DIM, _rope(kp[:, 0:pw], cq, s1q, s2q), onehot).astype(BF16)
    kwp_ref[...] = _rope(kp[:, pw:2 * pw], cq, s1q, s2q).astype(BF16)


def _project(x2d, norm_g, wq, wkv, wk_pad, tabs_q, tabs_k, *, prompt, seq):
    r, d = x2d.shape
    tm = TQ if prompt else r
    nt = seq // tm if prompt else 1
    row = lambda w: pl.BlockSpec((tm, w), lambda i: (i, 0))
    tab = pl.BlockSpec((tm, LANE), (lambda i: (i % nt, 0)) if prompt else (lambda i: (0, 0)))
    pw = N_KV * SLOT
    in_specs = [row(d), _const_spec((1, d)), _const_spec(wq.shape), _const_spec(wkv.shape)]
    args = [x2d, norm_g[None, :], wq, wkv]
    if prompt:
        in_specs.append(_const_spec(wk_pad.shape))
        args.append(wk_pad)
    tab_k = pl.BlockSpec((ROT_DIM // 2, tm), lambda i: (0, i % nt)) if prompt else tab
    in_specs += [tab] * 3 + [tab_k] * 3
    args += list(tabs_q) + list(tabs_k)
    if prompt:
        nbatch = r // seq
        qd = N_HEADS * HEAD_DIM
        tile_t = lambda rows: pl.BlockSpec((None, None, rows, tm), lambda i: (i // nt, i % nt, 0, 0))
        out_shape = [jax.ShapeDtypeStruct((nbatch, nt, qd, tm), BF16)] * 2
        out_specs = [tile_t(qd)] * 2
        out_shape += [jax.ShapeDtypeStruct((nbatch, KVROW, seq), F32)] * 3
        out_specs += [pl.BlockSpec((None, KVROW, tm), lambda i: (i // nt, 0, i % nt))] * 3
        out_shape += [jax.ShapeDtypeStruct((r, pw), BF16)] * 2
        out_specs += [row(pw)] * 2
        out_shape += [jax.ShapeDtypeStruct((nbatch, nt, pw, tm), BF16)] * 2
        out_specs += [tile_t(pw)] * 2
    else:
        qw = N_HEADS * SLOT
        out_shape = [jax.ShapeDtypeStruct((r, qw), BF16)] * 2
        out_specs = [row(qw), row(qw)]
        out_shape += [jax.ShapeDtypeStruct((r, KVROW), F32)] * 3
        out_specs += [row(KVROW)] * 3
    return pl.pallas_call(
        functools.partial(_proj_kernel, prompt=prompt, tm=tm, seq=seq),
        grid=(r // tm,), in_specs=in_specs, out_specs=tuple(out_specs), out_shape=tuple(out_shape),
        compiler_params=_params(("parallel",)),
        name="nsa_project_prompt" if prompt else "nsa_project_sample",
    )(*args)


def _cmp_lh_kernel(*refs, n_in, n_scalar):
    refs = refs[n_scalar:]
    x_refs = refs[:n_in]
    w_ref, lh_ref, xs_ref = refs[n_in], refs[n_in + 1], refs[n_in + 2]
    half = 2 * KVW
    tpb = x_refs[0].shape[-1]
    n_rows = lh_ref.shape[0]
    rows_u = n_rows // 2
    tiles_u = rows_u * CMP_STRIDE // LANE
    units = [(kv, hf) for kv in range(2) for hf in range(2)]

    pitch = xs_ref.shape[2] // CMP_STRIDE
    sub = 8
    eye_bf = jnp.where(lax.broadcasted_iota(jnp.int32, (LANE, LANE), 0)
                       == lax.broadcasted_iota(jnp.int32, (LANE, LANE), 1), 1.0, 0.0).astype(BF16)

    def fill(u):
        kv, hf = units[u]
        for tt in range(tiles_u):
            tok0 = (hf * tiles_u + tt) * LANE
            r, t0 = x_refs[tok0 // tpb], tok0 % tpb
            for gp in range(N_KV // 2):
                tile = jnp.concatenate([r[kv, 2 * gp, :, t0:t0 + LANE], r[kv, 2 * gp + 1, :, t0:t0 + LANE]], axis=0)
                tile_t = jnp.transpose(tile) if gp == 0 else _dot_nt(eye_bf, tile.astype(BF16))
                for v in range(LANE // sub):
                    j0 = (v * sub) % CMP_STRIDE
                    m = (tt * LANE + v * sub) // CMP_STRIDE
                    xs_ref[u, gp, pl.ds(j0 * pitch + m, sub, stride=pitch), :] = tile_t[v * sub:(v + 1) * sub, :]

    def matmuls(u):
        kv, hf = units[u]
        rows = slice(hf * rows_u, (hf + 1) * rows_u)
        for gp in range(N_KV // 2):
            x = jnp.concatenate([xs_ref[u, gp, j * pitch:j * pitch + rows_u, :] for j in range(CMP_STRIDE)],
                                axis=1).astype(BF16)
            acc = _dot(x, w_ref[kv])
            lh_ref[rows, kv * half + gp * LANE:kv * half + (gp + 1) * LANE] = acc[:, 0:LANE]
            lh_ref[rows, kv * half + KVW + gp * LANE:kv * half + KVW + (gp + 1) * LANE] = acc[:, LANE:2 * LANE]

    fill(0)
    for u in range(len(units)):
        if u + 1 < len(units):
            fill(u + 1)
        matmuls(u)


def _cmp_lh_prompt(kvc_t, wc):
    nbatch, _, _, _, seq = kvc_t.shape
    tt = min(seq, 2048)
    assert seq % tt == 0 and tt % LANE == 0
    nt = seq // tt
    out_w = 4 * KVW
    return pl.pallas_call(
        functools.partial(_cmp_lh_kernel, n_in=1, n_scalar=0),
        grid=(nbatch, nt),
        in_specs=[pl.BlockSpec((None, 2, N_KV, HEAD_DIM, tt), lambda b, i: (b, 0, 0, 0, i)), _const_spec(wc.shape)],
        out_specs=pl.BlockSpec((tt // CMP_STRIDE, out_w), lambda b, i: (b * nt + i, 0)),
        out_shape=jax.ShapeDtypeStruct((nbatch * seq // CMP_STRIDE, out_w), F32),
        scratch_shapes=[pltpu.VMEM((4, N_KV // 2, CMP_STRIDE * (tt // (2 * CMP_STRIDE) + 4), LANE), F32)],
        compiler_params=_params(("parallel", "parallel")), name="cmp_lh_prompt",
    )(kvc_t, wc)


def _cmp_lh_sample(pool_t, page_flat, wc):
    n_pages = page_flat.shape[0]
    pps = PAGES_PER_STEP
    assert n_pages % pps == 0
    steps = n_pages // pps
    blk = (None,) + tuple(pool_t.shape[1:])
    page_specs = [pl.BlockSpec(blk, (lambda i, pt, _p=p: (pt[i * pps + _p], 0, 0, 0, 0))) for p in range(pps)]
    nd = wc.ndim
    w_spec = pl.BlockSpec(wc.shape, lambda i, pt: (0,) * nd, pipeline_mode=pl.Buffered(1))
    out_w = 4 * KVW
    rows = pps * PAGE_SIZE // CMP_STRIDE
    grid_spec = pltpu.PrefetchScalarGridSpec(
        num_scalar_prefetch=1, grid=(steps,),
        in_specs=page_specs + [w_spec],
        out_specs=pl.BlockSpec((rows, out_w), lambda i, pt: (i, 0)),
        scratch_shapes=[pltpu.VMEM((4, N_KV // 2, CMP_STRIDE * (rows // 2 + 4), LANE), F32)])
    return pl.pallas_call(
        functools.partial(_cmp_lh_kernel, n_in=pps, n_scalar=1),
        grid_spec=grid_spec,
        out_shape=jax.ShapeDtypeStruct((steps * rows, out_w), F32),
        compiler_params=_params(("parallel",)), name="cmp_lh_sample",
    )(page_flat, *([pool_t] * pps), wc)


def _compress_finish(lh, pe_ref, w1f_ref, b1_ref, w2_ref, b2_ref, f32_v=False):
    nc = lh.shape[0]
    outs = []
    for kv in range(2):
        pe8 = jnp.broadcast_to(pe_ref[kv], (8, pe_ref.shape[-1])).astype(BF16)
        cb = _dot(pe8, w1f_ref[kv])[0:1, :] + b1_ref[kv]
        lo = lh[:, kv * 2 * KVW:kv * 2 * KVW + KVW]
        hi = lh[:, kv * 2 * KVW + KVW:(kv + 1) * 2 * KVW]
        hi_next = pltpu.roll(hi, nc - 1, axis=0)
        hid = _silu(lo + hi_next + cb)
        out = _dot(hid.astype(BF16), w2_ref[kv]) + b2_ref[kv]
        outs.append(out if (f32_v and kv == 1) else out.astype(BF16))
    return outs


def _select_blocks(imp, blk, jq, axis):
    forced = (blk == 0) | (blk == jq) | (blk == jq - 1)
    score = jnp.where((blk > jq) | (blk < 0), NEG_INF, jnp.where(forced, jnp.inf, imp))
    blk_f = blk.astype(F32)
    sel = jnp.zeros(imp.shape, F32)
    idxs = []
    for _ in range(N_SELECT):
        m = jnp.max(score, axis=axis, keepdims=True)
        idx = jnp.min(jnp.where(score == m, blk_f, 1e9), axis=axis, keepdims=True)
        hit = (blk_f == idx) & (m > NEG_INF)
        sel = jnp.where(hit, 1.0, sel)
        score = jnp.where(blk_f == idx, NEG_INF, score)
        idxs.append(idx)
    return sel, idxs


def _gate(xn, wz_ref, wgl_ref, gb_ref, e_ref):
    z = _dot(xn, wz_ref[...])
    gl = _dot(xn, wgl_ref[...]) + gb_ref[...]
    e = e_ref[...]
    a, b, c = _split3(gl)
    gle = _dot(a, e) + _dot(b, e) + _dot(c, e)
    return _silu(z), jax.nn.sigmoid(gle)


def _gate_kernel(x_ref, g_ref, wz_ref, wgl_ref, gb_ref, e_ref, sz_ref, gg_ref):
    xn = _rms(x_ref[...], g_ref[...]).astype(BF16)
    sz, gg = _gate(xn, wz_ref, wgl_ref, gb_ref, e_ref)
    sz_ref[...] = sz
    gg_ref[...] = gg


def _out_kernel(y_ref, x_ref, w_ref, g_ref, o_ref):
    o_ref[...] = _rms(_dot(y_ref[...].astype(BF16), w_ref[...]) + x_ref[...], g_ref[...])


def _attn_t_kernel(x_ref, qc_ref, qr_ref, lh_ref, ksa_ref, vst_ref,
                   kw0_ref, kw1_ref, kw2_ref, vw0_ref, vw1_ref, vw2_ref,
                   g1_ref, gf_ref, wz_ref, wgl_ref, gb_ref, wout_ref,
                   pe_ref, w1f_ref, b1_ref, w2_ref, b2_ref, ov_ref,
                   out_ref, kc_scr, vct_scr, o_scr, *, tq, seq):
    qi = pl.program_id(1)
    t0 = qi * tq
    nc = seq // CMP_STRIDE
    cols = Q_PER_KV * tq
    hd = HEAD_DIM

    @pl.when(qi == 0)
    def _():
        kc, vc = _compress_finish(lh_ref[...], pe_ref, w1f_ref, b1_ref, w2_ref, b2_ref, f32_v=True)
        kc_scr[...] = kc
        vct_scr[...] = jnp.transpose(vc).astype(BF16)

    a_col = lax.broadcasted_iota(jnp.int32, (1, cols), 1) % tq
    pos_col = t0 + a_col
    c_row = lax.broadcasted_iota(jnp.int32, (tq, 1), 0)
    n_row = lax.broadcasted_iota(jnp.int32, (nc, 1), 0)
    mask_c = (n_row * CMP_STRIDE + CMP_BLOCK - 1) <= pos_col
    causal = c_row <= a_col
    blk_t = lax.broadcasted_iota(jnp.int32, (SEL_BLOCK, tq), 0)
    jq_t = (t0 + lax.broadcasted_iota(jnp.int32, (SEL_BLOCK, tq), 1)) // SEL_BLOCK
    tiny = jnp.finfo(jnp.float32).tiny
    zpad = jnp.zeros((SLOT - hd, tq), BF16)

    def stack_t(ref, g, lower):
        return jnp.concatenate(
            [jnp.concatenate([ref[(Q_PER_KV * g + r) * hd:(Q_PER_KV * g + r + 1) * hd, :], lower], axis=0)
             for r in range(Q_PER_KV)], axis=1)

    def put(br, g, o_t):
        for r in range(Q_PER_KV):
            h = Q_PER_KV * g + r
            o_scr[br, h * hd:(h + 1) * hd, :] = o_t[:, r * tq:(r + 1) * tq]

    gsl = lambda g: slice(g * SLOT, (g + 1) * SLOT)

    def cmp_scores(g, n):
        return _dot(kc_scr[0:n, gsl(g)], stack_t(qc_ref, g, zpad))

    def cmp_finish(g, s, n):
        s = jnp.where(mask_c[0:n], s, NEG_INF)
        m = jnp.max(s, axis=0, keepdims=True)
        m = jnp.where(m > NEG_INF, m, 0.0)
        e = jnp.exp2(s - m)
        p = e * (1.0 / jnp.maximum(jnp.sum(e, axis=0, keepdims=True), tiny))
        put(0, g, _dot(vct_scr[gsl(g), 0:n], p.astype(BF16))[0:hd])
        psum = p[:, 0:tq] + p[:, tq:2 * tq] + p[:, 2 * tq:3 * tq] + p[:, 3 * tq:4 * tq]
        ov = ov_ref[:, 0:n]
        pa, pb, pc = _split3(psum)
        return _dot(ov, pa) + _dot(ov, pb) + _dot(ov, pc)


    def win_scores(g):
        qr_t = stack_t(qr_ref, g, zpad)
        return (_dot(kw0_ref[:, gsl(g)], qr_t), _dot(kw1_ref[:, gsl(g)], qr_t), _dot(kw2_ref[:, gsl(g)], qr_t))

    def win_finish(g, ss):
        s0 = jnp.where((c_row > a_col) & (qi >= 2), ss[0], NEG_INF)
        s1 = jnp.where(qi >= 1, ss[1], NEG_INF)
        s2 = jnp.where(causal, ss[2], NEG_INF)
        mw = jnp.maximum(jnp.maximum(jnp.max(s0, axis=0, keepdims=True), jnp.max(s1, axis=0, keepdims=True)),
                         jnp.max(s2, axis=0, keepdims=True))
        acc_w = (_dot(vw0_ref[gsl(g), :], jnp.exp2(s0 - mw).astype(BF16))
                 + _dot(vw1_ref[gsl(g), :], jnp.exp2(s1 - mw).astype(BF16))
                 + _dot(vw2_ref[gsl(g), :], jnp.exp2(s2 - mw).astype(BF16)))
        put(2, g, acc_w[0:hd] * (1.0 / acc_w[hd:hd + 1]))

    x = x_ref[...]
    xn_t = jnp.transpose(_rms(x, g1_ref[...])).astype(BF16)
    qw = N_HEADS * hd
    imps = [None] * N_KV
    z_raw = [None] * N_BRANCH
    s_c = cmp_scores(0, nc)
    for g in range(N_KV):
        s_next = cmp_scores(g + 1, nc) if g + 1 < N_KV else win_scores(0)
        if g < N_BRANCH:
            z_raw[g] = _dot(wz_ref[g * qw:(g + 1) * qw, :], xn_t)
        else:
            gl_raw = _dot(wgl_ref[...], xn_t)
        imps[g] = cmp_finish(g, s_c, nc)
        s_c = s_next
    gate = jax.nn.sigmoid(gl_raw + jnp.concatenate([gb_ref[...]] * (tq // LANE), axis=1))
    szs = [_silu(z) for z in z_raw]
    sel_all, _ = _select_blocks(jnp.concatenate(imps, axis=1), jnp.concatenate([blk_t] * N_KV, axis=1),
                                jnp.concatenate([jq_t] * N_KV, axis=1), axis=0)
    maskbias_all = ((sel_all - 1.0) * MASK_BIG).astype(BF16)
    s_w = s_c
    for g in range(N_KV):
        s_next = win_scores(g + 1) if g + 1 < N_KV else None
        win_finish(g, s_w)
        s_w = s_next

    qa = [stack_t(qr_ref, g, maskbias_all[:, g * tq:(g + 1) * tq]) for g in range(N_KV)]

    def scores(g, kt):
        start = pl.multiple_of(kt * tq, tq)
        return _dot(ksa_ref[pl.ds(start, tq), gsl(g)], qa[g])

    def consume(g, kt, st, m_i, acc):
        m_new = jnp.maximum(m_i, jnp.max(st, axis=0, keepdims=True))
        alpha = jnp.exp2(m_i - m_new)
        pt = jnp.exp2(st - m_new).astype(BF16)
        return m_new, alpha * acc + _dot(vst_ref[kt, gsl(g), :], pt)

    def all_groups(kt, carry, diag):
        out = []
        st = scores(0, kt)
        for g in range(N_KV):
            st_next = scores(g + 1, kt) if g + 1 < N_KV else None
            if diag:
                st = jnp.where(causal, st, -MASK_BIG)
            out.extend(consume(g, kt, st, carry[2 * g], carry[2 * g + 1]))
            st = st_next
        return tuple(out)

    init = (jnp.full((1, cols), NEG_INF, F32), jnp.zeros((SLOT, cols), F32)) * N_KV
    carry = lax.fori_loop(0, qi // 2,
                          lambda j, c: all_groups(2 * j + 1, all_groups(2 * j, c, False), False), init)
    carry = lax.cond(qi % 2 == 1, lambda c: all_groups(qi - 1, c, False), lambda c: c, carry)
    carry = all_groups(qi, carry, True)
    for g in range(N_KV):
        acc_s = carry[2 * g + 1]
        put(1, g, acc_s[0:hd] * (1.0 / acc_s[hd:hd + 1]))

    y_t = None
    for br in range(N_BRANCH):
        sz = szs[br]
        parts = []
        for h in range(N_HEADS):
            j = br * N_HEADS + h
            parts.append(o_scr[br, h * hd:(h + 1) * hd, :] * sz[h * hd:(h + 1) * hd, :] * gate[j:j + 1, :])
        t = jnp.concatenate(parts, axis=0)
        y_t = t if y_t is None else y_t + t
    o = jnp.transpose(_dot(wout_ref[...], y_t.astype(BF16)))
    out_ref[...] = _rms(o + x, gf_ref[...])


def _attn_prompt_t(x1, qc_t, qr_t, lh, ksa, vst, kwp, vwt, g1, gf, wz_t, wgl_t, gb_b, wout_t,
                   pe, w1f, b1, w2p, b2p, ov, *, batch, seq):
    tq = TQ
    assert seq % tq == 0 and WINDOW == 2 * tq and seq % SEL_BLOCK == 0 and seq // SEL_BLOCK <= SEL_BLOCK
    nq = seq // tq
    nc = seq // CMP_STRIDE
    d = x1.shape[1]
    pw = N_KV * SLOT
    qd = N_HEADS * HEAD_DIM
    tile = lambda w: pl.BlockSpec((tq, w), lambda b, i: (b * nq + i, 0))
    tile_t = lambda rows: pl.BlockSpec((None, None, rows, tq), lambda b, i: (b, i, 0, 0))
    per_b = lambda rows, w: pl.BlockSpec((rows, w), lambda b, i: (b, 0), pipeline_mode=pl.Buffered(1))
    vst_spec = pl.BlockSpec((None, nq, pw, tq), lambda b, i: (b, 0, 0, 0), pipeline_mode=pl.Buffered(1))
    kwin = lambda off: pl.BlockSpec((tq, pw), lambda b, i: (b * nq + jnp.maximum(i - off, 0), 0))
    vwin = lambda off: pl.BlockSpec((None, None, pw, tq), lambda b, i: (b, jnp.maximum(i - off, 0), 0, 0))
    consts = [g1[None, :], gf[None, :], wz_t, wgl_t, gb_b, wout_t, pe, w1f, b1, w2p, b2p, ov]
    in_specs = ([tile(d), tile_t(qd), tile_t(qd), per_b(nc, lh.shape[1]), per_b(seq, pw), vst_spec,
                 kwin(2), kwin(1), kwin(0), vwin(2), vwin(1), vwin(0)]
                + [_const_spec(c.shape) for c in consts])
    return pl.pallas_call(
        functools.partial(_attn_t_kernel, tq=tq, seq=seq),
        grid=(batch, nq), in_specs=in_specs, out_specs=tile(d),
        out_shape=jax.ShapeDtypeStruct((batch * seq, d), F32),
        scratch_shapes=[pltpu.VMEM((nc, pw), BF16), pltpu.VMEM((pw, nc), BF16),
                        pltpu.VMEM((N_BRANCH, qd, tq), F32)],
        compiler_params=_params(("arbitrary", "arbitrary")), name="nsa_attend_prompt",
    )(x1, qc_t, qr_t, lh, ksa, vst, kwp, kwp, kwp, vwt, vwt, vwt, *consts)


def _sample_cmp_kernel(lh_ref, qc_ref, pe_ref, w1f_ref, b1_ref, w2_ref, b2_ref, ov_ref,
                       imp_ref, oc_ref, *, past):
    kc, vc = _compress_finish(lh_ref[...], pe_ref, w1f_ref, b1_ref, w2_ref, b2_ref)
    nc = kc.shape[0]
    nbp = ov_ref.shape[1]
    q = qc_ref[...]
    head_row = lax.broadcasted_iota(jnp.int32, (N_HEADS, 1), 0)
    n_col = lax.broadcasted_iota(jnp.int32, (1, nc), 1)
    mask_c = (n_col * CMP_STRIDE + CMP_BLOCK - 1) <= past
    row8 = lax.broadcasted_iota(jnp.int32, (8, 1), 0)
    tiny = jnp.finfo(jnp.float32).tiny
    oc = jnp.zeros((N_HEADS, SLOT), F32)
    psum8 = jnp.zeros((8, nc), F32)
    for g in range(N_KV):
        gs = slice(g * SLOT, (g + 1) * SLOT)
        in_g = (head_row // Q_PER_KV) == g
        s = jnp.where(mask_c, _dot_nt(q, kc[:, gs]), NEG_INF)
        m = jnp.max(s, axis=-1, keepdims=True)
        m = jnp.where(m > NEG_INF, m, 0.0)
        e = jnp.where(mask_c, jnp.exp(s - m), 0.0)
        p = e / jnp.maximum(jnp.sum(e, axis=-1, keepdims=True), tiny)
        oc = jnp.where(in_g, _dot(p.astype(BF16), vc[:, gs]), oc)
        pg = jnp.sum(jnp.where(in_g, p, 0.0), axis=0, keepdims=True)
        psum8 = jnp.where(row8 == g, pg, psum8)
    ov = ov_ref[...]
    pa, pb, pc = _split3(psum8)
    imp_ref[...] = _dot(pa, ov) + _dot(pb, ov) + _dot(pc, ov)
    oc_ref[...] = oc


def _sample_topk_kernel(imp_ref, topi_ref, *, past):
    imp_t = jnp.transpose(imp_ref[...])
    blk = lax.broadcasted_iota(jnp.int32, imp_t.shape, 0)
    blk = jnp.where(blk < past // SEL_BLOCK + 1, blk, -1)
    _, idxs = _select_blocks(imp_t, blk, jnp.int32(past // SEL_BLOCK), axis=0)
    topi_ref[...] = jnp.concatenate(idxs, axis=0).astype(jnp.int32)


def _sample_cmp(lh3, qc3, pe, w1f, b1, w2p, b2p, ov, *, past):
    nb, nc, w = lh3.shape
    nbp = ov.shape[1]
    consts = [pe, w1f, b1, w2p, b2p, ov]
    imp, oc = pl.pallas_call(
        functools.partial(_sample_cmp_kernel, past=past),
        grid=(nb,),
        in_specs=[pl.BlockSpec((None, nc, w), lambda b: (b, 0, 0)),
                  pl.BlockSpec((None, N_HEADS, SLOT), lambda b: (b, 0, 0))]
                 + [_const_spec(c.shape) for c in consts],
        out_specs=(pl.BlockSpec((None, 8, nbp), lambda b: (b, 0, 0)),
                   pl.BlockSpec((None, N_HEADS, SLOT), lambda b: (b, 0, 0))),
        out_shape=(jax.ShapeDtypeStruct((nb, 8, nbp), F32),
                   jax.ShapeDtypeStruct((nb, N_HEADS, SLOT), F32)),
        compiler_params=_params(("parallel",)), name="nsa_sample_cmp",
    )(lh3, qc3, *consts)
    rows = nb * 8
    topi = pl.pallas_call(
        functools.partial(_sample_topk_kernel, past=past), grid=(1,),
        in_specs=[_const_spec((rows, nbp), single=False)],
        out_specs=_const_spec((N_SELECT, rows), single=False),
        out_shape=jax.ShapeDtypeStruct((N_SELECT, rows), jnp.int32),
        compiler_params=_params(("arbitrary",)), name="nsa_sample_topk",
    )(imp.reshape(rows, nbp))
    topi = jnp.transpose(topi.reshape(N_SELECT, nb, 8)[:, :, :N_KV], (1, 2, 0))
    return topi, oc


def _sample_attn_kernel(pt_ref, ti_ref, *refs, n_pool_blk, wlen):
    kv_refs = refs[:N_SELECT]
    (qr_ref, ksn_ref, vsn_ref, kwn_ref, vwn_ref, win_ref, wnew_ref, oc_ref, sz_ref, gg_ref,
     y_ref, wout_ref) = refs[N_SELECT:]
    b = pl.program_id(0)
    g = pl.program_id(1)
    per_page = PAGE_SIZE // SEL_BLOCK
    q = qr_ref[...]
    qf = q[:, 0:HEAD_DIM].astype(F32)
    in_g = (lax.broadcasted_iota(jnp.int32, (N_HEADS, 1), 0) // Q_PER_KV) == g
    zpad = jnp.zeros((SLOT - HEAD_DIM, PAGE_SIZE), BF16)
    tok_half = lax.broadcasted_iota(jnp.int32, (1, PAGE_SIZE), 1) // SEL_BLOCK

    s_new = jnp.sum(qf * ksn_ref[...], axis=-1, keepdims=True)
    scores = []
    m = s_new
    for r in range(N_SELECT):
        t = ti_ref[(b * N_KV + g) * N_SELECT + r]
        ok = (tok_half == t % per_page) & (t < n_pool_blk)
        k_t = jnp.concatenate([kv_refs[r][0].astype(BF16), zpad], axis=0)
        s = jnp.where(ok, _dot(q, k_t), NEG_INF)
        scores.append(s)
        m = jnp.maximum(m, jnp.max(s, axis=-1, keepdims=True))
    p_new = jnp.exp(s_new - m)
    l = p_new
    o = p_new * vsn_ref[...]
    for r in range(N_SELECT):
        p = jnp.exp(scores[r] - m)
        l = l + jnp.sum(p, axis=-1, keepdims=True)
        o = o + _dot_nt(p.astype(BF16), kv_refs[r][1].astype(BF16))
    o_s = o / l

    wkey = lax.broadcasted_iota(jnp.int32, (1, wlen), 1)
    kw_t = jnp.concatenate([win_ref[0].astype(BF16), jnp.zeros((SLOT - HEAD_DIM, wlen), BF16)], axis=0)
    s = jnp.where(wkey > wlen - WINDOW, _dot(q, kw_t), NEG_INF)
    s_new = jnp.sum(qf * kwn_ref[...], axis=-1, keepdims=True)
    m = jnp.maximum(jnp.max(s, axis=-1, keepdims=True), s_new)
    p = jnp.exp(s - m)
    p_new = jnp.exp(s_new - m)
    l = jnp.sum(p, axis=-1, keepdims=True) + p_new
    o_w = (_dot_nt(p.astype(BF16), win_ref[1].astype(BF16)) + p_new * vwn_ref[...]) / l

    o_c = oc_ref[:, 0:HEAD_DIM]
    y = o_c * sz_ref[0] * gg_ref[0] + o_s * sz_ref[1] * gg_ref[1] + o_w * sz_ref[2] * gg_ref[2]

    @pl.when(g == 0)
    def _():
        y_ref[...] = jnp.zeros(y_ref.shape, F32)

    y_ref[...] = jnp.where(in_g, y, y_ref[...])
    for kv in range(2):
        shifted = pltpu.roll(win_ref[kv], wlen - 1, axis=1)
        wout_ref[kv] = jnp.where(wkey == wlen - 1, wnew_ref[kv], shifted)


def _sample_attn(page_flat, topi_flat, pool_t, qr3, ksn, vsn, kwn, vwn, win_t, wnew, oc, sz4, gg4, *, n_pages):
    nb = qr3.shape[0]
    wlen = win_t.shape[-1]
    per_page = PAGE_SIZE // SEL_BLOCK
    n_pool_blk = n_pages * per_page

    def pool_map(r):
        def f(b, g, pt, ti):
            blk = jnp.minimum(ti[(b * N_KV + g) * N_SELECT + r], n_pool_blk - 1)
            return (pt[b * n_pages + blk // per_page], 0, g, 0, 0)
        return f

    b3 = lambda s1, s2: pl.BlockSpec((None, s1, s2), lambda b, g, pt, ti: (b, 0, 0))
    b4 = pl.BlockSpec((None, N_BRANCH, N_HEADS, HEAD_DIM), lambda b, g, pt, ti: (b, 0, 0, 0))
    page = lambda r: pl.BlockSpec((None, 2, None, HEAD_DIM, PAGE_SIZE), pool_map(r))
    wspec = lambda last: pl.BlockSpec((None, 2, None, HEAD_DIM, last), lambda b, g, pt, ti: (b, 0, g, 0, 0))
    in_specs = ([page(r) for r in range(N_SELECT)]
                + [b3(N_HEADS, SLOT)] + [b3(N_HEADS, HEAD_DIM)] * 4
                + [wspec(wlen), wspec(1), b3(N_HEADS, SLOT), b4, b4])
    grid_spec = pltpu.PrefetchScalarGridSpec(
        num_scalar_prefetch=2, grid=(nb, N_KV), in_specs=in_specs,
        out_specs=(b3(N_HEADS, HEAD_DIM), wspec(wlen)))
    return pl.pallas_call(
        functools.partial(_sample_attn_kernel, n_pool_blk=n_pool_blk, wlen=wlen),
        grid_spec=grid_spec,
        out_shape=(jax.ShapeDtypeStruct((nb, N_HEADS, HEAD_DIM), F32),
                   jax.ShapeDtypeStruct(win_t.shape, F32)),
        compiler_params=_params(("arbitrary", "arbitrary")), name="nsa_sample_sel_win",
    )(page_flat, topi_flat, *([pool_t] * N_SELECT), qr3, ksn, vsn, kwn, vwn, win_t, wnew, oc, sz4, gg4)


def _rope_tables(pos):
    half = ROT_DIM // 2
    freqs = jnp.exp(-math.log(ROPE_THETA) * jnp.arange(half, dtype=F32) * (2.0 / ROT_DIM))
    ang = pos.astype(F32)[:, None] * freqs[None, :]
    cos, sin = jnp.cos(ang), jnp.sin(ang)
    n = pos.shape[0]
    one = jnp.ones((n, HEAD_DIM - ROT_DIM), F32)
    zero8 = jnp.zeros((n, half), F32)
    zrest = jnp.zeros((n, HEAD_DIM - ROT_DIM), F32)
    c64 = jnp.concatenate([cos, cos, one], axis=1)
    s1_64 = jnp.concatenate([-sin, zero8, zrest], axis=1)
    s2_64 = jnp.concatenate([zero8, sin, zrest], axis=1)
    pad1 = jnp.ones((n, SLOT - HEAD_DIM), F32)
    pad0 = jnp.zeros((n, SLOT - HEAD_DIM), F32)
    tabs_q = (jnp.concatenate([c64, pad1], axis=1), jnp.concatenate([s1_64, pad0], axis=1),
              jnp.concatenate([s2_64, pad0], axis=1))
    tabs_k = tuple(jnp.concatenate([t, t], axis=1) for t in (c64, s1_64, s2_64))
    tabs_t = (jnp.transpose(cos), jnp.transpose(sin), jnp.transpose(cos))
    return tabs_q, tabs_k, tabs_t


def _pad_heads(w, n_heads):
    k = w.shape[0]
    w3 = w.reshape(k, n_heads, HEAD_DIM)
    return jnp.pad(w3, ((0, 0), (0, 0), (0, SLOT - HEAD_DIM))).reshape(k, n_heads * SLOT)


def _overlap(nc, nb):
    n = np.arange(nc)[:, None] * CMP_STRIDE
    j = np.arange(nb)[None, :] * SEL_BLOCK
    return ((n <= j + SEL_BLOCK - 1) & (n + CMP_BLOCK - 1 >= j)).astype(np.float32)


def kernel(x_prompt, x_sample, cache_cmp_kv, cache_sel_kv, state_win_kv, page_table, norm_g, final_norm_g,
           a_w_in, a_ln_g, a_ln_b, a_w_s, a_b_s, a_w_out, b_w_in, b_cmp_pe, b_cmp_w1, b_cmp_b1, b_cmp_w2,
           b_cmp_b2, b_gate_b, b_w_out):
    batch, seq, d = x_prompt.shape
    nb, dec_seq, _ = x_sample.shape
    assert dec_seq == 1
    n_pages = page_table.shape[1]
    past = n_pages * PAGE_SIZE
    assert past % SEL_BLOCK == 0 and past % CMP_STRIDE == 0 and past // SEL_BLOCK + 1 >= N_SELECT
    wlen = state_win_kv.shape[2]
    qw = N_HEADS * HEAD_DIM

    a_win = a_w_in[0].astype(BF16)
    a_wout = a_w_out[0].astype(BF16)
    xp1 = _layer_a(x_prompt.reshape(batch * seq, d), norm_g[0], a_win, a_ln_g[0], a_ln_b[0], a_w_s[0], a_b_s[0],
                   a_wout, sample=False)
    xs1, chunk_v = _layer_a(x_sample.reshape(nb, d), norm_g[0], a_win, a_ln_g[0], a_ln_b[0], a_w_s[0], a_b_s[0],
                            a_wout, sample=True)

    w_in = b_w_in[0]
    o1 = qw
    o2 = o1 + N_BRANCH * KVROW
    o3 = o2 + N_BRANCH * qw
    wq_pad = _pad_heads(w_in[:, :o1], N_HEADS).astype(BF16)
    wkv = w_in[:, o1:o2].astype(BF16)
    wk_sel = w_in[:, o1 + KVROW:o1 + KVROW + KVW]
    wk_win = w_in[:, o1 + 2 * KVROW:o1 + 2 * KVROW + KVW]
    wk_pad = jnp.concatenate([_pad_heads(w, N_KV) for w in (wk_sel, wk_win)], axis=1).astype(BF16)
    wz = w_in[:, o2:o3].astype(BF16)
    n_gate = N_BRANCH * N_HEADS
    n_gate_pad = -(-n_gate // 16) * 16
    wgl = jnp.pad(w_in[:, o3:], ((0, 0), (0, LANE - n_gate))).astype(BF16)
    gb = jnp.pad(b_gate_b[0].reshape(1, n_gate), ((0, 0), (0, LANE - n_gate)))
    wz_t = jnp.transpose(w_in[:, o2:o3]).astype(BF16)
    wgl_t = jnp.pad(jnp.transpose(w_in[:, o3:]), ((0, n_gate_pad - n_gate), (0, 0))).astype(BF16)
    gb_b = jnp.broadcast_to(jnp.pad(b_gate_b[0].reshape(n_gate), (0, n_gate_pad - n_gate))[:, None],
                            (n_gate_pad, LANE))
    wout_t = jnp.transpose(b_w_out[0]).astype(BF16)
    e_np = np.zeros((LANE, n_gate * HEAD_DIM), np.float32)
    for j in range(n_gate):
        e_np[j, j * HEAD_DIM:(j + 1) * HEAD_DIM] = 1.0
    e_mat = jnp.asarray(e_np, dtype=BF16)
    wout = b_w_out[0].astype(BF16)
    w1 = b_cmp_w1[0]
    eye = jnp.eye(N_KV, dtype=F32)
    w1r = w1.reshape(2, 2, CMP_STRIDE, HEAD_DIM, HEAD_DIM)
    wc = jnp.einsum("Gg,kljch->kjGclgh", jnp.eye(2, dtype=F32), w1r)
    wc = wc.reshape(2, CMP_STRIDE * LANE, 2 * LANE).astype(BF16)
    pe = b_cmp_pe[0].reshape(2, 1, CMP_BLOCK * HEAD_DIM)
    w1f = jnp.tile(w1.reshape(2, CMP_BLOCK * HEAD_DIM, HEAD_DIM), (1, 1, N_KV)).astype(BF16)
    b1 = jnp.tile(b_cmp_b1[0], (1, N_KV)).reshape(2, 1, KVW)
    w2bd = jnp.einsum("Gg,kch->kGcgh", eye, b_cmp_w2[0])
    w2p = jnp.pad(w2bd, ((0, 0),) * 4 + ((0, SLOT - HEAD_DIM),)).reshape(2, KVW, N_KV * SLOT).astype(BF16)
    b2p = jnp.pad(jnp.broadcast_to(b_cmp_b2[0][:, None, :], (2, N_KV, HEAD_DIM)),
                  ((0, 0), (0, 0), (0, SLOT - HEAD_DIM))).reshape(2, 1, N_KV * SLOT)

    tabs_q, _, tabs_t = _rope_tables(jnp.arange(seq, dtype=jnp.int32))
    wq_t = jnp.transpose(w_in[:, :o1]).astype(BF16)
    wkv_t = jnp.transpose(w_in[:, o1:o2]).astype(BF16)
    (qc_t, qr_t, kvc_t, kvs_t, kvw_t, ksa, kwp, vst, vwt) = _project(xp1, norm_g[1], wq_t, wkv_t, wk_pad, tabs_q,
                                                                    tabs_t, prompt=True, seq=seq)
    fm = lambda t: t.reshape(t.shape[0], 2, N_KV, HEAD_DIM, t.shape[-1])
    lh_p = _cmp_lh_prompt(fm(kvc_t), wc)
    ov_t = np.zeros((SEL_BLOCK, seq // CMP_STRIDE), np.float32)
    ov_t[:seq // SEL_BLOCK] = _overlap(seq // CMP_STRIDE, seq // SEL_BLOCK).T
    y_prompt = _attn_prompt_t(xp1, qc_t, qr_t, lh_p, ksa, vst, kwp, vwt, norm_g[1], final_norm_g, wz_t, wgl_t, gb_b,
                              wout_t, pe, w1f, b1, w2p, b2p, jnp.asarray(ov_t, dtype=BF16), batch=batch, seq=seq)

    pos_s = jnp.full((nb,), past, dtype=jnp.int32)
    tabs_qs, tabs_ks, _ = _rope_tables(pos_s)
    qc_s, qr_s, kvc_s, kvs_s, kvw_s = _project(xs1, norm_g[1], wq_pad, wkv, None, tabs_qs, tabs_ks,
                                               prompt=False, seq=1)
    page_flat = page_table.reshape(-1).astype(jnp.int32)
    n_phys = cache_cmp_kv.shape[1]
    to_fm = lambda t: jnp.transpose(t, (0, 2, 3, 4, 1))
    lh_s = _cmp_lh_sample(to_fm(cache_cmp_kv[0]), page_flat, wc)
    nc_s = past // CMP_STRIDE
    nb_blk = past // SEL_BLOCK + 1
    nbp = -(-nb_blk // LANE) * LANE
    ov_np = np.zeros((nc_s, nbp), np.float32)
    ov_np[:, :nb_blk] = _overlap(nc_s, nb_blk)
    topi, oc_s = _sample_cmp(lh_s.reshape(nb, nc_s, lh_s.shape[1]), qc_s.reshape(nb, N_HEADS, SLOT),
                             pe, w1f, b1, w2p, b2p, jnp.asarray(ov_np, dtype=BF16), past=past)
    topi_flat = topi.reshape(-1)
    r_gate = xs1.shape[0]
    sz_s, gg_s = pl.pallas_call(
        _gate_kernel, grid=(1,),
        in_specs=[_const_spec(s, single=False) for s in
                  ((r_gate, d), (1, d), wz.shape, wgl.shape, gb.shape, e_mat.shape)],
        out_specs=(_const_spec((r_gate, N_BRANCH * qw), single=False),) * 2,
        out_shape=(jax.ShapeDtypeStruct((r_gate, N_BRANCH * qw), F32),) * 2,
        compiler_params=_params(("arbitrary",)), name="nsa_gate_sample",
    )(xs1, norm_g[1][None, :], wz, wgl, gb, e_mat)
    expand = lambda t: jnp.repeat(t.reshape(nb, N_KV, HEAD_DIM), Q_PER_KV, axis=1)
    y_heads, win_s_t = _sample_attn(
        page_flat, topi_flat, to_fm(cache_sel_kv[0]), qr_s.reshape(nb, N_HEADS, SLOT),
        expand(kvs_s[:, :KVW]), expand(kvs_s[:, KVW:]), expand(kvw_s[:, :KVW]), expand(kvw_s[:, KVW:]),
        to_fm(state_win_kv[0]), kvw_s.reshape(nb, 2, N_KV, HEAD_DIM, 1), oc_s,
        sz_s.reshape(nb, N_BRANCH, N_HEADS, HEAD_DIM), gg_s.reshape(nb, N_BRANCH, N_HEADS, HEAD_DIM),
        n_pages=n_pages)
    y_sample = pl.pallas_call(
        _out_kernel, grid=(1,),
        in_specs=[_const_spec(s, single=False) for s in ((nb, qw), (nb, d), wout.shape, (1, d))],
        out_specs=_const_spec((nb, d), single=False),
        out_shape=jax.ShapeDtypeStruct((nb, d), F32),
        compiler_params=_params(("arbitrary",)), name="nsa_out_sample",
    )(y_heads.reshape(nb, qw), xs1, wout, final_norm_g[None, :])

    kv6 = lambda t: t.reshape(1, nb, 1, 2, N_KV, HEAD_DIM)
    from_fm = lambda t: jnp.transpose(t, (0, 4, 1, 2, 3))[None]
    wl_p = min(WINDOW, seq)
    return (y_prompt.reshape(batch, seq, d), y_sample.reshape(nb, 1, d),
            from_fm(fm(kvc_t)), kv6(kvc_s), from_fm(fm(kvs_t)), kv6(kvs_s),
            from_fm(fm(kvw_t)[..., seq - wl_p:]), from_fm(win_s_t), chunk_v.reshape(1, nb, 1, -1))
```
